```python
import math
import jax, jax.numpy as jnp
from jax import lax
import numpy as np

D_MODEL = 1024
BATCH = 8
SEQ = 8192
DEPTH = 2

CHUNK = 64
Q_BLOCK = 128
ROPE_THETA = 10000.0
EPS = 1e-6

HG_HEADS = 4
HG_DIM = 128
HG_WIDTH = HG_HEADS * HG_DIM
AT_HEADS = 8
AT_KV_HEADS = 2
AT_DIM = 64
AT_WIDTH = AT_HEADS * AT_DIM
KV_WIDTH = AT_KV_HEADS * AT_DIM
IDX_HEADS = 4
IDX_DIM = 64
TOPK_MAX = 256
EVEN_IN = 4 * HG_WIDTH + AT_WIDTH + 2 * KV_WIDTH + IDX_HEADS * IDX_DIM + IDX_DIM + IDX_HEADS
MIX_WIDTH = HG_WIDTH + AT_WIDTH

SSM_INNER = 2 * D_MODEL
SSM_HEAD_DIM = 64
SSM_HEADS = SSM_INNER // SSM_HEAD_DIM
SSM_GROUPS = 4
SSM_STATE = 128
SSM_CONV = 4
SSM_CONV_CH = SSM_INNER + 2 * SSM_GROUPS * SSM_STATE
ODD_IN = SSM_INNER + SSM_CONV_CH + SSM_HEADS

MOE_GROUPS = 4
MOE_PER_GROUP = 8
MOE_EXPERTS = MOE_GROUPS * MOE_PER_GROUP
MOE_TOPK = 2
MOE_FF = 512

N_EVEN = (DEPTH + 1) // 2
N_ODD = DEPTH // 2

kernel_name = "hybrid_hgrn2_dsa_mamba2_hmoe"


def _offsets(sizes):
    out, acc = [], 0
    for s in sizes[:-1]:
        acc += s
        out.append(acc)
    return out


def rmsnorm(x, g):
    xf = x.astype(jnp.float32)
    y = xf * lax.rsqrt(jnp.mean(xf * xf, axis=-1, keepdims=True) + EPS)
    return (y * g.astype(jnp.float32)).astype(x.dtype)


def rope(x, pos):
    half = x.shape[-1] // 2
    inv = jnp.exp(-math.log(ROPE_THETA) * jnp.arange(half, dtype=jnp.float32) / half)
    ang = pos.astype(jnp.float32)[:, None] * inv[None, :]
    cos = jnp.cos(ang)[:, None, :]
    sin = jnp.sin(ang)[:, None, :]
    xf = x.astype(jnp.float32)
    x1, x2 = xf[..., :half], xf[..., half:]
    return jnp.concatenate([x1 * cos - x2 * sin, x2 * cos + x1 * sin], axis=-1).astype(x.dtype)


def hgrn2_scan(q, k, v, logf):
    B, S, H, K = q.shape
    V = v.shape[-1]
    n = S // CHUNK

    def to_chunks(a):
        return a.astype(jnp.float32).reshape(B, n, CHUNK, H, a.shape[-1]).transpose(1, 0, 3, 2, 4)

    qc, kc, vc, gc = to_chunks(q), to_chunks(k), to_chunks(v), to_chunks(logf)
    causal = jnp.tril(jnp.ones((CHUNK, CHUNK), dtype=bool))

    def step(state, inp):
        qi, ki, vi, gi = inp
        b = jnp.cumsum(gi, axis=2)
        diff = b[:, :, :, None, :] - b[:, :, None, :, :]
        decay = jnp.exp(jnp.where(causal[:, :, None], diff, -jnp.inf))
        attn = jnp.einsum('bhtk,bhsk,bhtsk->bhts', qi, ki, decay)
        o = jnp.einsum('bhts,bhsv->bhtv', attn, vi)
        o = o + jnp.einsum('bhtk,bhkv->bhtv', qi * jnp.exp(b), state)
        b_last = b[:, :, -1:, :]
        state = state * jnp.exp(b_last)[:, :, 0, :, None] + jnp.einsum(
            'bhsk,bhsv->bhkv', ki * jnp.exp(b_last - b), vi)
        return state, o

    state0 = jnp.zeros((B, H, K, V), jnp.float32)
    _, o = lax.scan(step, state0, (qc, kc, vc, gc))
    return o.transpose(1, 0, 3, 2, 4).reshape(B, S, H, V)


def dsa_attention(q, k, v, iq, ik, iw):
    B, S = q.shape[0], q.shape[1]
    topk = min(TOPK_MAX, S // 4)
    nb = S // Q_BLOCK
    grp = AT_HEADS // AT_KV_HEADS
    key_chunk = jnp.arange(S) // CHUNK
    ikf = ik.astype(jnp.float32)

    def blk(args):
        qb, iqb, iwb, qchunk = args
        sc = jnp.einsum('bqhd,bsd->bqhs', iqb.astype(jnp.float32), ikf) * (IDX_DIM ** -0.5)
        score = jnp.einsum('bqh,bqhs->bqs', iwb.astype(jnp.float32), jax.nn.relu(sc))
        adm = key_chunk[None, :] <= qchunk[:, None]
        score = jnp.where(adm[None], score, -jnp.inf)
        _, idx = lax.top_k(score, topk)
        valid = key_chunk[idx] <= qchunk[None, :, None]
        ks = jax.vmap(lambda a, i: a[i])(k, idx)
        vs = jax.vmap(lambda a, i: a[i])(v, idx)
        qg = qb.reshape(B, Q_BLOCK, AT_KV_HEADS, grp, AT_DIM).astype(jnp.float32)
        s = jnp.einsum('bqhgd,bqnhd->bqhgn', qg, ks.astype(jnp.float32)) * (AT_DIM ** -0.5)
        s = jnp.where(valid[:, :, None, None, :], s, -jnp.inf)
        p = jax.nn.softmax(s, axis=-1)
        o = jnp.einsum('bqhgn,bqnhd->bqhgd', p, vs.astype(jnp.float32))
        return o.reshape(B, Q_BLOCK, AT_WIDTH).astype(q.dtype)

    qb = q.reshape(B, nb, Q_BLOCK, AT_HEADS, AT_DIM).transpose(1, 0, 2, 3, 4)
    iqb = iq.reshape(B, nb, Q_BLOCK, IDX_HEADS, IDX_DIM).transpose(1, 0, 2, 3, 4)
    iwb = iw.reshape(B, nb, Q_BLOCK, IDX_HEADS).transpose(1, 0, 2, 3)
    qchunks = key_chunk.reshape(nb, Q_BLOCK)
    out = lax.map(blk, (qb, iqb, iwb, qchunks))
    return out.transpose(1, 0, 2, 3).reshape(B, S, AT_WIDTH)


def even_mixer(h, w_in, lb, hg_norm, q_norm, k_norm, w_out, pos):
    B, S, _ = h.shape
    sizes = [HG_WIDTH] * 4 + [AT_WIDTH, KV_WIDTH, KV_WIDTH, IDX_HEADS * IDX_DIM, IDX_DIM, IDX_HEADS]
    proj = h @ w_in
    hq, hf, hi, hg, aq, ak, av, iq, ik, iw = jnp.split(proj, _offsets(sizes), axis=-1)
    hf32 = hf.astype(jnp.float32)
    lb32 = lb.astype(jnp.float32)
    f = lb32 + (1.0 - lb32) * jax.nn.sigmoid(hf32)
    kk = (1.0 - lb32) * jax.nn.sigmoid(-hf32)
    hshape = (B, S, HG_HEADS, HG_DIM)
    o_h = hgrn2_scan(hq.reshape(hshape), kk.reshape(hshape), hi.reshape(hshape), jnp.log(f).reshape(hshape))
    o_h = rmsnorm(o_h, hg_norm.reshape(HG_HEADS, HG_DIM)) * jax.nn.silu(hg.reshape(hshape).astype(jnp.float32))
    o_h = o_h.reshape(B, S, HG_WIDTH).astype(h.dtype)
    q = rope(rmsnorm(aq.reshape(B, S, AT_HEADS, AT_DIM), q_norm), pos)
    k = rope(rmsnorm(ak.reshape(B, S, AT_KV_HEADS, AT_DIM), k_norm), pos)
    v = av.reshape(B, S, AT_KV_HEADS, AT_DIM)
    iq = rope(iq.reshape(B, S, IDX_HEADS, IDX_DIM), pos)
    ik = rope(ik.reshape(B, S, 1, IDX_DIM), pos)[:, :, 0]
    o_a = dsa_attention(q, k, v, iq, ik, iw)
    o = jnp.concatenate([o_h, o_a], axis=-1)
    return o @ w_out


def causal_conv(x, w, b):
    C = x.shape[-1]
    y = lax.conv_general_dilated(x, w[:, None, :].astype(x.dtype), window_strides=(1,),
                                 padding=[(SSM_CONV - 1, 0)],
                                 dimension_numbers=('NWC', 'WIO', 'NWC'),
                                 feature_group_count=C)
    return y + b


def ssd_scan(x, dt, A, Bm, Cm):
    B, S, H, P = x.shape
    G, N = Bm.shape[2], Bm.shape[3]
    R = H // G
    n = S // CHUNK
    xdt = (x.astype(jnp.float32) * dt[..., None]).reshape(B, n, CHUNK, G, R, P).transpose(1, 0, 2, 3, 4, 5)
    dA = (dt * A.astype(jnp.float32)).reshape(B, n, CHUNK, G, R).transpose(1, 0, 2, 3, 4)
    bc = Bm.astype(jnp.float32).reshape(B, n, CHUNK, G, N).transpose(1, 0, 2, 3, 4)
    cc = Cm.astype(jnp.float32).reshape(B, n, CHUNK, G, N).transpose(1, 0, 2, 3, 4)
    causal = jnp.tril(jnp.ones((CHUNK, CHUNK), dtype=bool))

    def step(state, inp):
        xi, ai, bi, ci = inp
        cum = jnp.cumsum(ai, axis=1)
        seg = cum[:, :, None] - cum[:, None, :]
        L = jnp.exp(jnp.where(causal[None, :, :, None, None], seg, -jnp.inf))
        cb = jnp.einsum('btgn,bsgn->btsg', ci, bi)
        y = jnp.einsum('btsg,btsgr,bsgrp->btgrp', cb, L, xi)
        y = y + jnp.einsum('btgn,bgrpn->btgrp', ci, state) * jnp.exp(cum)[..., None]
        to_end = jnp.exp(cum[:, -1:] - cum)
        state = state * jnp.exp(cum[:, -1])[..., None, None] + jnp.einsum(
            'bsgn,bsgr,bsgrp->bgrpn', bi, to_end, xi)
        return state, y

    state0 = jnp.zeros((B, G, R, P, N), jnp.float32)
    _, y = lax.scan(step, state0, (xdt, dA, bc, cc))
    return y.transpose(1, 0, 2, 3, 4, 5).reshape(B, S, H, P)


def odd_mixer(h, w_in, conv_w, conv_b, dt_bias, A_log, D_skip, norm_g, w_out):
    B, S, _ = h.shape
    proj = h @ w_in
    z, xBC, dt = jnp.split(proj, [SSM_INNER, SSM_INNER + SSM_CONV_CH], axis=-1)
    xBC = jax.nn.silu(causal_conv(xBC, conv_w, conv_b))
    xs, Bm, Cm = jnp.split(xBC, [SSM_INNER, SSM_INNER + SSM_GROUPS * SSM_STATE], axis=-1)
    dt = jax.nn.softplus(dt.astype(jnp.float32) + dt_bias.astype(jnp.float32))
    A = -jnp.exp(A_log.astype(jnp.float32))
    xs = xs.reshape(B, S, SSM_HEADS, SSM_HEAD_DIM)
    y = ssd_scan(xs, dt, A, Bm.reshape(B, S, SSM_GROUPS, SSM_STATE), Cm.reshape(B, S, SSM_GROUPS, SSM_STATE))
    y = y + D_skip.astype(jnp.float32)[:, None] * xs.astype(jnp.float32)
    y = y.reshape(B, S, SSM_INNER) * jax.nn.silu(z.astype(jnp.float32))
    gsz = SSM_INNER // SSM_GROUPS
    y = rmsnorm(y.reshape(B, S, SSM_GROUPS, gsz), norm_g.reshape(SSM_GROUPS, gsz))
    return y.reshape(B, S, SSM_INNER).astype(h.dtype) @ w_out


def hier_moe(h, wg, bg, we, be, w1, w3, w2):
    B, S, D = h.shape
    t = h.reshape(-1, D)
    T = t.shape[0]
    gprob = jax.nn.softmax((t @ wg).astype(jnp.float32) + bg.astype(jnp.float32), axis=-1)
    gval, gidx = lax.top_k(gprob, 1)
    elog = ((t @ we).astype(jnp.float32) + be.astype(jnp.float32)).reshape(T, MOE_GROUPS, MOE_PER_GROUP)
    elog = jnp.take_along_axis(elog, gidx[:, :, None], axis=1)[:, 0]
    eprob = jax.nn.softmax(elog, axis=-1)
    ev, ei = lax.top_k(eprob, MOE_TOPK)
    ev = ev / jnp.sum(ev, axis=-1, keepdims=True)
    weight = gval * ev
    flat = gidx * MOE_PER_GROUP + ei
    gate = jnp.sum(jax.nn.one_hot(flat, MOE_EXPERTS, dtype=jnp.float32) * weight[..., None], axis=1)
    y = jnp.zeros((T, D), jnp.float32)
    for e in range(MOE_EXPERTS):
        he = jax.nn.silu(t @ w1[e]) * (t @ w3[e])
        y = y + gate[:, e:e + 1] * (he @ w2[e]).astype(jnp.float32)
    return y.reshape(B, S, D).astype(h.dtype)


def setup_inputs(seed: int = 0) -> dict:
    key = jax.random.key(seed)
    ks = jax.random.split(key, 24)
    nrm = jax.random.normal
    f32 = jnp.float32
    D = D_MODEL
    x = nrm(ks[0], (BATCH, SEQ, D), f32)
    mix_norm = 1.0 + 0.05 * nrm(ks[1], (DEPTH, D), f32)
    ffn_norm = 1.0 + 0.05 * nrm(ks[2], (DEPTH, D), f32)
    ev_w_in = nrm(ks[3], (N_EVEN, D, EVEN_IN), f32) * D ** -0.5
    hg_lb_logits = 0.5 * nrm(ks[4], (N_EVEN + 1, HG_WIDTH), f32)
    hg_out_norm = 1.0 + 0.05 * nrm(ks[5], (N_EVEN, HG_WIDTH), f32)
    at_q_norm = 1.0 + 0.05 * nrm(ks[6], (N_EVEN, AT_DIM), f32)
    at_k_norm = 1.0 + 0.05 * nrm(ks[7], (N_EVEN, AT_DIM), f32)
    ev_w_out = nrm(ks[8], (N_EVEN, MIX_WIDTH, D), f32) * (0.5 * MIX_WIDTH ** -0.5)
    od_w_in = nrm(ks[9], (N_ODD, D, ODD_IN), f32) * D ** -0.5
    od_conv_w = nrm(ks[10], (N_ODD, SSM_CONV, SSM_CONV_CH), f32) * SSM_CONV ** -0.5
    od_conv_b = 0.02 * nrm(ks[11], (N_ODD, SSM_CONV_CH), f32)
    dt0 = jnp.exp(jax.random.uniform(ks[12], (N_ODD, SSM_HEADS), f32) * (math.log(0.1) - math.log(0.001)) + math.log(0.001))
    od_dt_bias = dt0 + jnp.log(-jnp.expm1(-dt0))
    od_A_log = jnp.log(jax.random.uniform(ks[13], (N_ODD, SSM_HEADS), f32, 1.0, 16.0))
    od_D = 1.0 + 0.1 * nrm(ks[14], (N_ODD, SSM_HEADS), f32)
    od_out_norm = 1.0 + 0.05 * nrm(ks[15], (N_ODD, SSM_INNER), f32)
    od_w_out = nrm(ks[16], (N_ODD, SSM_INNER, D), f32) * (0.5 * SSM_INNER ** -0.5)
    moe_wg = nrm(ks[17], (DEPTH, D, MOE_GROUPS), f32) * D ** -0.5
    moe_bg = 0.01 * nrm(ks[18], (DEPTH, MOE_GROUPS), f32)
    moe_we = nrm(ks[19], (DEPTH, D, MOE_EXPERTS), f32) * D ** -0.5
    moe_be = 0.01 * nrm(ks[20], (DEPTH, MOE_EXPERTS), f32)
    moe_w1 = nrm(ks[21], (DEPTH, MOE_EXPERTS, D, MOE_FF), f32) * D ** -0.5
    moe_w3 = nrm(ks[22], (DEPTH, MOE_EXPERTS, D, MOE_FF), f32) * D ** -0.5
    moe_w2 = nrm(ks[23], (DEPTH, MOE_EXPERTS, MOE_FF, D), f32) * MOE_FF ** -0.5
    return {"x": x, "mix_norm": mix_norm, "ffn_norm": ffn_norm,
            "ev_w_in": ev_w_in, "hg_lb_logits": hg_lb_logits, "hg_out_norm": hg_out_norm,
            "at_q_norm": at_q_norm, "at_k_norm": at_k_norm, "ev_w_out": ev_w_out,
            "od_w_in": od_w_in, "od_conv_w": od_conv_w, "od_conv_b": od_conv_b,
            "od_dt_bias": od_dt_bias, "od_A_log": od_A_log, "od_D": od_D,
            "od_out_norm": od_out_norm, "od_w_out": od_w_out,
            "moe_wg": moe_wg, "moe_bg": moe_bg, "moe_we": moe_we, "moe_be": moe_be,
            "moe_w1": moe_w1, "moe_w3": moe_w3, "moe_w2": moe_w2}


def reference(x, mix_norm, ffn_norm, ev_w_in, hg_lb_logits, hg_out_norm, at_q_norm, at_k_norm,
              ev_w_out, od_w_in, od_conv_w, od_conv_b, od_dt_bias, od_A_log, od_D, od_out_norm,
              od_w_out, moe_wg, moe_bg, moe_we, moe_be, moe_w1, moe_w3, moe_w2):
    pos = jnp.arange(x.shape[1], dtype=jnp.int32)
    lbs = jnp.cumsum(jax.nn.softmax(hg_lb_logits.astype(jnp.float32), axis=0), axis=0)
    for layer in range(DEPTH):
        i = layer // 2
        h = rmsnorm(x, mix_norm[layer])
        if layer % 2 == 0:
            x = x + even_mixer(h, ev_w_in[i], lbs[i], hg_out_norm[i], at_q_norm[i], at_k_norm[i],
                               ev_w_out[i], pos)
        else:
            x = x + odd_mixer(h, od_w_in[i], od_conv_w[i], od_conv_b[i], od_dt_bias[i], od_A_log[i],
                              od_D[i], od_out_norm[i], od_w_out[i])
        h = rmsnorm(x, ffn_norm[layer])
        x = x + hier_moe(h, moe_wg[layer], moe_bg[layer], moe_we[layer], moe_be[layer],
                         moe_w1[layer], moe_w3[layer], moe_w2[layer])
    return x
```

```python
import functools
import math

import jax
import jax.numpy as jnp
from jax import lax
from jax.experimental import pallas as pl
from jax.experimental.pallas import tpu as pltpu

D_MODEL = 1024
CHUNK = 64
Q_BLOCK = 128
ROPE_THETA = 10000.0
EPS = 1e-6

HG_HEADS = 4
HG_DIM = 128
HG_WIDTH = HG_HEADS * HG_DIM
AT_HEADS = 8
AT_KV_HEADS = 2
AT_GROUP = AT_HEADS // AT_KV_HEADS
AT_DIM = 64
AT_WIDTH = AT_HEADS * AT_DIM
KV_WIDTH = AT_KV_HEADS * AT_DIM
IDX_HEADS = 4
IDX_DIM = 64
TOPK_MAX = 256

SSM_INNER = 2 * D_MODEL
SSM_HEAD_DIM = 64
SSM_HEADS = SSM_INNER // SSM_HEAD_DIM
SSM_GROUPS = 4
SSM_STATE = 128
SSM_CONV = 4
SSM_CONV_CH = SSM_INNER + 2 * SSM_GROUPS * SSM_STATE
SSM_GROUP_WIDTH = SSM_INNER // SSM_GROUPS

MOE_GROUPS = 4
MOE_PER_GROUP = 8
MOE_EXPERTS = MOE_GROUPS * MOE_PER_GROUP
MOE_FF = 512

LANES = 128
SUBLANES = 8
VMEM_LIMIT = 56 * 1024 * 1024
ROW_TILE = 256
KEY_TILE = 512
HG_BLOCK = 512
SSD_BLOCK = 256
EXPERT_TILE = 512
ROUTE_ROWS = 8 + MOE_EXPERTS
INT_MIN = -(2 ** 31)
NEG_BIG = -1e30

F32 = jnp.float32
BF16 = jnp.bfloat16
HIGHEST = lax.Precision.HIGHEST


def _params(*sem):
    return pltpu.CompilerParams(dimension_semantics=sem, vmem_limit_bytes=VMEM_LIMIT)


def _dot(a, b):
    return jnp.dot(a, b, preferred_element_type=F32)


def _dot_f32(a, b):
    return jnp.dot(a, b, preferred_element_type=F32, precision=HIGHEST)


def _dot_nt(a, b, precision=None):
    return lax.dot_general(a, b, (((1,), (1,)), ((), ())), preferred_element_type=F32, precision=precision)


def _dot_tn(a, b, precision=None):
    return lax.dot_general(a, b, (((0,), (0,)), ((), ())), preferred_element_type=F32, precision=precision)


def _sigmoid(x):
    return 1.0 / (1.0 + jnp.exp(-x))


def _silu(x):
    return x * _sigmoid(x)


def _rms_rows(x, gain):
    return x * lax.rsqrt(jnp.mean(x * x, axis=-1, keepdims=True) + EPS) * gain


def _rope_slab(x, cos, sin_lo, sin_hi):
    return x * cos + pltpu.roll(x, LANES - 32, 1) * sin_lo + pltpu.roll(x, 32, 1) * sin_hi


def _even_proj_kernel(x_ref, g_ref, wh_ref, wq_ref, wkv_ref, wiq_ref, wik_ref,
                      r1_ref, gm_ref, qn_ref, kn_ref,
                      h_ref, q_ref, kv_ref, iq_ref, ik_ref, iw_ref):
    xn = _rms_rows(x_ref[...], g_ref[...]).astype(BF16)
    h_ref[...] = _dot(xn, wh_ref[...])

    cos1, lo1, hi1 = r1_ref[0], r1_ref[1], r1_ref[2]
    gm = gm_ref[...]

    def qk_norm(v, gain):
        ms = _dot_f32(v * v, gm)
        return v * lax.rsqrt(ms + EPS) * gain

    q = _dot(xn, wq_ref[...])
    for h in range(AT_HEADS):
        sl = slice(h * LANES, (h + 1) * LANES)
        qh = _rope_slab(qk_norm(q[:, sl], qn_ref[...]), cos1, lo1, hi1)
        q_ref[:, sl] = (qh * (AT_DIM ** -0.5)).astype(BF16)
    kv = _dot(xn, wkv_ref[...])
    for h in range(AT_KV_HEADS):
        sl = slice(h * LANES, (h + 1) * LANES)
        kv_ref[:, sl] = _rope_slab(qk_norm(kv[:, sl], kn_ref[...]), cos1, lo1, hi1).astype(BF16)
    kv_ref[:, AT_KV_HEADS * LANES:] = kv[:, AT_KV_HEADS * LANES:].astype(BF16)
    iq = _dot(xn, wiq_ref[...])
    for h in range(IDX_HEADS):
        sl = slice(h * LANES, (h + 1) * LANES)
        iq_ref[:, sl] = (_rope_slab(iq[:, sl], cos1, lo1, hi1) * (IDX_DIM ** -0.5)).astype(BF16)
    ikw = _rope_slab(_dot(xn, wik_ref[...]), cos1, lo1, hi1)
    ik_ref[...] = ikw.astype(BF16)
    iw_ref[...] = ikw


def _pad_heads(w, heads):
    d = w.shape[0]
    w = w.reshape(d, heads, 64)
    return jnp.pad(w, ((0, 0), (0, 0), (0, 64))).reshape(d, heads * LANES)


def _rope_tables(seq):
    half = AT_DIM // 2
    inv = jnp.exp(-math.log(ROPE_THETA) * jnp.arange(half, dtype=F32) / half)
    ang = jnp.arange(seq, dtype=F32)[:, None] * inv[None, :]
    c, s = jnp.cos(ang), jnp.sin(ang)
    z, o = jnp.zeros_like(c), jnp.ones_like(c)
    return jnp.stack([jnp.concatenate([c, c, o, o], 1), jnp.concatenate([-s, z, z, z], 1),
                      jnp.concatenate([z, s, z, z], 1)])


def _even_proj(x2d, gain, w_in, q_norm, k_norm, seq):
    t = x2d.shape[0]
    tm = ROW_TILE
    o = [0, 4 * HG_WIDTH]
    for wdt in (AT_WIDTH, KV_WIDTH, KV_WIDTH, IDX_HEADS * IDX_DIM, IDX_DIM, IDX_HEADS):
        o.append(o[-1] + wdt)
    wh = w_in[:, o[0]:o[1]].astype(BF16)
    wq = _pad_heads(w_in[:, o[1]:o[2]], AT_HEADS).astype(BF16)
    wkv = jnp.concatenate([_pad_heads(w_in[:, o[2]:o[3]], AT_KV_HEADS),
                           _pad_heads(w_in[:, o[3]:o[4]], AT_KV_HEADS)], 1).astype(BF16)
    wiq = _pad_heads(w_in[:, o[4]:o[5]], IDX_HEADS).astype(BF16)
    wik = jnp.pad(w_in[:, o[5]:o[7]], ((0, 0), (0, LANES - IDX_DIM - IDX_HEADS))).astype(BF16)
    rope1 = _rope_tables(seq)
    lane = jnp.arange(LANES)
    gm = jnp.where((lane[:, None] < AT_DIM) & (lane[None, :] < AT_DIM), 1.0 / AT_DIM, 0.0).astype(F32)
    pad_gain = lambda g: jnp.pad(g, (0, LANES - AT_DIM), constant_values=1.0).reshape(1, LANES)
    nseq = seq // tm
    row = lambda i: (i, 0)
    fixed = lambda i: (0, 0)
    rope_map = lambda i: (0, i % nseq, 0)
    outs = pl.pallas_call(
        _even_proj_kernel,
        grid=(t // tm,),
        in_specs=[pl.BlockSpec((tm, D_MODEL), row), pl.BlockSpec((1, D_MODEL), fixed),
                  pl.BlockSpec(wh.shape, fixed), pl.BlockSpec(wq.shape, fixed), pl.BlockSpec(wkv.shape, fixed),
                  pl.BlockSpec(wiq.shape, fixed), pl.BlockSpec(wik.shape, fixed),
                  pl.BlockSpec((3, tm, LANES), rope_map),
                  pl.BlockSpec((LANES, LANES), fixed), pl.BlockSpec((1, LANES), fixed),
                  pl.BlockSpec((1, LANES), fixed)],
        out_specs=[pl.BlockSpec((tm, 4 * HG_WIDTH), row), pl.BlockSpec((tm, AT_HEADS * LANES), row),
                   pl.BlockSpec((tm, 2 * AT_KV_HEADS * LANES), row), pl.BlockSpec((tm, IDX_HEADS * LANES), row),
                   pl.BlockSpec((tm, LANES), row), pl.BlockSpec((tm, LANES), row)],
        out_shape=[jax.ShapeDtypeStruct((t, 4 * HG_WIDTH), F32), jax.ShapeDtypeStruct((t, AT_HEADS * LANES), BF16),
                   jax.ShapeDtypeStruct((t, 2 * AT_KV_HEADS * LANES), BF16),
                   jax.ShapeDtypeStruct((t, IDX_HEADS * LANES), BF16),
                   jax.ShapeDtypeStruct((t, LANES), BF16), jax.ShapeDtypeStruct((t, LANES), F32)],
        compiler_params=_params("parallel"),
        name="even_proj",
    )(x2d, gain.reshape(1, D_MODEL), wh, wq, wkv, wiq, wik, rope1, gm, pad_gain(q_norm), pad_gain(k_norm))
    return outs


HG_SUB = 16


def _hgrn_kernel(q_ref, f_ref, i_ref, g_ref, lb_ref, gn_ref, tri_ref, o_ref, st_ref, od_ref):
    @pl.when(pl.program_id(2) == 0)
    def _():
        st_ref[...] = jnp.zeros_like(st_ref)

    lb = lb_ref[...]
    tri = tri_ref[...]
    s_idx = lax.broadcasted_iota(jnp.int32, (HG_SUB, 1), 0)

    def chunk(c, carry):
        r0 = pl.multiple_of(c * CHUNK, CHUNK)
        rows = pl.ds(r0, CHUNK)
        hf = f_ref[rows, :]
        q = q_ref[rows, :]
        v = i_ref[rows, :]
        f = lb + (1.0 - lb) * _sigmoid(hf)
        kk = (1.0 - lb) * _sigmoid(-hf)
        b = _dot_f32(tri, jnp.log(f))
        st = st_ref[...]
        o = _dot_nt((q * jnp.exp(b)).astype(BF16), st.astype(BF16))
        off = []
        for blk in range(CHUNK // HG_SUB):
            lo, hi = blk * HG_SUB, (blk + 1) * HG_SUB
            bs, ks, vs = b[lo:hi], kk[lo:hi], v[lo:hi]
            for tt in range(HG_SUB):
                t = lo + tt
                e = jnp.exp(jnp.minimum(b[t:t + 1] - bs, 0.0))
                a = jnp.sum(e * (q[t:t + 1] * ks), axis=-1, keepdims=True)
                a = jnp.where(s_idx <= tt, a, 0.0)
                od_ref[t:t + 1, :] = jnp.sum(a * vs, axis=0, keepdims=True)
            if blk == 0:
                off.append(jnp.zeros((HG_SUB, HG_DIM), F32))
            else:
                ref = b[lo - 1:lo]
                qd = (q[lo:hi] * jnp.exp(b[lo:hi] - ref)).astype(BF16)
                kd = (kk[:lo] * jnp.exp(ref - b[:lo])).astype(BF16)
                off.append(_dot(_dot_nt(qd, kd).astype(BF16), v[:lo].astype(BF16)))
        o = o + od_ref[...] + jnp.concatenate(off, axis=0)
        b_last = b[CHUNK - 1:CHUNK, :]
        st_ref[...] = st * jnp.exp(b_last) + _dot_tn(v.astype(BF16), (kk * jnp.exp(b_last - b)).astype(BF16))
        o = _rms_rows(o, gn_ref[...]) * _silu(g_ref[rows, :])
        o_ref[rows, :] = o.astype(o_ref.dtype)
        return carry

    lax.fori_loop(0, HG_BLOCK // CHUNK, chunk, 0)


def _hgrn(hproj, lb, out_norm, batch, seq):
    t = hproj.shape[0]
    nblk = seq // HG_BLOCK
    tri = jnp.tril(jnp.ones((CHUNK, CHUNK), F32))

    def col(k):
        return lambda b, h, s: (b * nblk + s, k * HG_HEADS + h)

    head = lambda b, h, s: (0, h)
    return pl.pallas_call(
        _hgrn_kernel,
        grid=(batch, HG_HEADS, nblk),
        in_specs=[pl.BlockSpec((HG_BLOCK, HG_DIM), col(0)), pl.BlockSpec((HG_BLOCK, HG_DIM), col(1)),
                  pl.BlockSpec((HG_BLOCK, HG_DIM), col(2)), pl.BlockSpec((HG_BLOCK, HG_DIM), col(3)),
                  pl.BlockSpec((1, HG_DIM), head), pl.BlockSpec((1, HG_DIM), head),
                  pl.BlockSpec((CHUNK, CHUNK), lambda b, h, s: (0, 0))],
        out_specs=pl.BlockSpec((HG_BLOCK, HG_DIM), lambda b, h, s: (b * nblk + s, h)),
        out_shape=jax.ShapeDtypeStruct((t, HG_WIDTH), BF16),
        scratch_shapes=[pltpu.VMEM((HG_DIM, HG_DIM), F32), pltpu.VMEM((CHUNK, HG_DIM), F32)],
        compiler_params=_params("parallel", "parallel", "arbitrary"),
        name="hgrn2_scan",
    )(hproj, hproj, hproj, hproj, lb.reshape(1, HG_WIDTH), out_norm.reshape(1, HG_WIDTH), tri)


def _dsa_kernel(q_ref, iq_ref, iw_ref, kv_ref, ik_ref, o_ref, keys_ref, *, topk, idx_bits):
    j = pl.program_id(1)
    kb = KEY_TILE
    nkt = (j * Q_BLOCK + Q_BLOCK + kb - 1) // kb
    row = lax.broadcasted_iota(jnp.int32, (Q_BLOCK, 1), 0)
    nadm = ((j * Q_BLOCK + row) // CHUNK + 1) * CHUNK
    lane_col = lax.broadcasted_iota(jnp.int32, (Q_BLOCK, kb), 1)
    iw = iw_ref[...]
    iws = [iw[:, IDX_DIM + h:IDX_DIM + h + 1] for h in range(IDX_HEADS)]

    def score_tile(kt, carry):
        ik = ik_ref[kt]
        score = jnp.zeros((Q_BLOCK, kb), F32)
        for h in range(IDX_HEADS):
            sc = _dot_nt(iq_ref[:, h * LANES:(h + 1) * LANES], ik)
            score = score + iws[h] * jnp.maximum(sc, 0.0)
        bits = pltpu.bitcast(score, jnp.int32)
        key = jnp.where(bits < 0, bits ^ 0x7FFFFFFF, bits)
        key = jnp.where(bits == INT_MIN, 0, key)
        key = jnp.where(kt * kb + lane_col < nadm, key, INT_MIN)
        keys_ref[kt] = key
        return carry

    lax.fori_loop(0, nkt, score_tile, 0)

    def lane_fold(c):
        acc = c[:, 0:LANES]
        for i in range(1, kb // LANES):
            acc = acc + c[:, i * LANES:(i + 1) * LANES]
        return acc

    def count(indicator):
        def body(kt, acc):
            return acc + lane_fold(indicator(keys_ref[kt], kt))
        acc = lax.fori_loop(0, nkt, body, jnp.zeros((Q_BLOCK, LANES), F32))
        return jnp.sum(acc, axis=1, keepdims=True)

    def value_bit(i, tu):
        cand = tu | jnp.left_shift(jnp.int32(1), 31 - i)
        cand_s = cand ^ INT_MIN
        cnt = count(lambda ks, kt: jnp.where(ks >= cand_s, 1.0, 0.0))
        return jnp.where(cnt >= topk, cand, tu)

    thr = lax.fori_loop(0, 32, value_bit, jnp.zeros((Q_BLOCK, 1), jnp.int32)) ^ INT_MIN
    need = topk - count(lambda ks, kt: jnp.where(ks > thr, 1.0, 0.0))

    def index_bit(i, jj):
        cand = jj | jnp.left_shift(jnp.int32(1), idx_bits - 1 - i)
        cnt = count(lambda ks, kt: jnp.where(ks == thr, jnp.where(kt * kb + lane_col < cand, 1.0, 0.0), 0.0))
        return jnp.where(cnt < need, cand, jj)

    last_tie = lax.fori_loop(0, idx_bits, index_bit, jnp.zeros((Q_BLOCK, 1), jnp.int32))

    rows4 = AT_GROUP * Q_BLOCK
    qs = [jnp.concatenate([q_ref[:, (g * AT_GROUP + hh) * LANES:(g * AT_GROUP + hh + 1) * LANES]
                           for hh in range(AT_GROUP)], axis=0) for g in range(AT_KV_HEADS)]

    def att_tile(kt, carry):
        ks = keys_ref[kt]
        col = kt * kb + lane_col
        tie = jnp.where(ks == thr, jnp.where(col <= last_tie, 0.0, NEG_BIG), NEG_BIG)
        bias = jnp.where(col < nadm, jnp.where(ks > thr, 0.0, tie), NEG_BIG)
        bias4 = jnp.concatenate([bias] * AT_GROUP, axis=0)
        kv = kv_ref[kt]
        out = []
        for g in range(AT_KV_HEADS):
            m, l, acc = carry[g]
            s = _dot_nt(qs[g], kv[:, g * LANES:(g + 1) * LANES]) + bias4
            m_new = jnp.maximum(m, jnp.max(s, axis=-1, keepdims=True))
            alpha = jnp.exp(m - m_new)
            p = jnp.exp(s - m_new)
            l = alpha * l + jnp.sum(p, axis=-1, keepdims=True)
            vg = kv[:, (AT_KV_HEADS + g) * LANES:(AT_KV_HEADS + g + 1) * LANES]
            acc = alpha * acc + _dot(p.astype(BF16), vg)
            out.append((m_new, l, acc))
        return tuple(out)

    init = tuple((jnp.full((rows4, 1), 0.1 * NEG_BIG, F32), jnp.zeros((rows4, 1), F32),
                  jnp.zeros((rows4, LANES), F32)) for _ in range(AT_KV_HEADS))
    res = lax.fori_loop(0, nkt, att_tile, init)
    for g in range(AT_KV_HEADS):
        _, l, acc = res[g]
        og = acc / l
        for hh in range(AT_GROUP):
            h = g * AT_GROUP + hh
            o_ref[:, h * LANES:(h + 1) * LANES] = og[hh * Q_BLOCK:(hh + 1) * Q_BLOCK].astype(o_ref.dtype)


def _dsa(q, iq, iw, kv, ik, batch, seq):
    t = q.shape[0]
    nb = seq // Q_BLOCK
    nkt = seq // KEY_TILE
    topk = min(TOPK_MAX, seq // 4)
    kv3 = kv.reshape(batch * nkt, KEY_TILE, kv.shape[1])
    ik3 = ik.reshape(batch * nkt, KEY_TILE, LANES)
    qrow = lambda b, j: (b * nb + j, 0)
    seq_blk = lambda b, j: (b, 0, 0)
    return pl.pallas_call(
        functools.partial(_dsa_kernel, topk=topk, idx_bits=max(1, (seq - 1).bit_length())),
        grid=(batch, nb),
        in_specs=[pl.BlockSpec((Q_BLOCK, AT_HEADS * LANES), qrow), pl.BlockSpec((Q_BLOCK, IDX_HEADS * LANES), qrow),
                  pl.BlockSpec((Q_BLOCK, LANES), qrow),
                  pl.BlockSpec((nkt, KEY_TILE, kv.shape[1]), seq_blk),
                  pl.BlockSpec((nkt, KEY_TILE, LANES), seq_blk)],
        out_specs=pl.BlockSpec((Q_BLOCK, AT_HEADS * LANES), qrow),
        out_shape=jax.ShapeDtypeStruct((t, AT_HEADS * LANES), BF16),
        scratch_shapes=[pltpu.VMEM((nkt, Q_BLOCK, KEY_TILE), jnp.int32)],
        compiler_params=_params("parallel", "arbitrary"),
        name="dsa_attention",
    )(q, iq, iw, kv3, ik3)


def _out_router_kernel(*refs, n_in):
    x_ref = refs[0]
    a_refs = refs[1:1 + n_in]
    w_refs = refs[1 + n_in:1 + 2 * n_in]
    g_ref, wr_ref, br_ref, su_ref = refs[1 + 2 * n_in:5 + 2 * n_in]
    x1_ref, hn_ref, route_ref, cnt_ref = refs[5 + 2 * n_in:9 + 2 * n_in]
    carry_ref = refs[9 + 2 * n_in]

    @pl.when(pl.program_id(0) == 0)
    def _():
        carry_ref[...] = jnp.zeros_like(carry_ref)

    x1 = x_ref[...]
    for a_ref, w_ref in zip(a_refs, w_refs):
        x1 = x1 + _dot(a_ref[...], w_ref[...])
    x1_ref[...] = x1
    hn = _rms_rows(x1, g_ref[...])
    hn_ref[...] = hn

    lt = _dot_nt(wr_ref[...], hn, precision=HIGHEST) + br_ref[...]
    tm = lt.shape[1]
    r4 = lax.broadcasted_iota(jnp.int32, (MOE_GROUPS, tm), 0).astype(F32)
    gl = lt[0:MOE_GROUPS]
    gmax = jnp.max(gl, axis=0, keepdims=True)
    gidx = jnp.min(jnp.where(gl == gmax, r4, float(MOE_GROUPS)), axis=0, keepdims=True)
    gval = 1.0 / jnp.sum(jnp.exp(gl - gmax), axis=0, keepdims=True)
    el = jnp.zeros((MOE_PER_GROUP, tm), F32)
    for g in range(MOE_GROUPS):
        el = el + jnp.where(gidx == float(g), lt[8 + g * MOE_PER_GROUP:8 + (g + 1) * MOE_PER_GROUP], 0.0)
    r8 = lax.broadcasted_iota(jnp.int32, (MOE_PER_GROUP, tm), 0).astype(F32)
    m1 = jnp.max(el, axis=0, keepdims=True)
    i1 = jnp.min(jnp.where(el == m1, r8, float(MOE_PER_GROUP)), axis=0, keepdims=True)
    el2 = jnp.where(r8 == i1, -jnp.inf, el)
    m2 = jnp.max(el2, axis=0, keepdims=True)
    i2 = jnp.min(jnp.where(el2 == m2, r8, float(MOE_PER_GROUP)), axis=0, keepdims=True)
    e21 = jnp.exp(m2 - m1)
    wa = gval / (1.0 + e21)
    wb = gval * e21 / (1.0 + e21)
    fa = gidx * MOE_PER_GROUP + i1
    fb = gidx * MOE_PER_GROUP + i2
    re = lax.broadcasted_iota(jnp.int32, (MOE_EXPERTS, tm), 0).astype(F32)
    hit_a = re == fa
    hit_b = re == fb
    member = jnp.where(hit_a, 1.0, jnp.where(hit_b, 1.0, 0.0))
    before = _dot(member.astype(BF16), su_ref[...]) + carry_ref[...]
    rank_a = jnp.sum(jnp.where(hit_a, before, 0.0), axis=0, keepdims=True)
    rank_b = jnp.sum(jnp.where(hit_b, before, 0.0), axis=0, keepdims=True)
    carry_ref[...] = carry_ref[...] + jnp.sum(member, axis=1, keepdims=True)
    cnt_ref[...] = jnp.broadcast_to(carry_ref[...], cnt_ref.shape)
    zero = jnp.zeros_like(wa)
    route_ref[...] = jnp.concatenate([fa, fb, wa, wb, rank_a, rank_b, zero, zero], axis=0)


def _out_router(x2d, acts, weights, ffn_gain, wg, bg, we, be):
    t = x2d.shape[0]
    tm = ROW_TILE
    n_in = len(acts)
    wr = jnp.concatenate([wg.T, jnp.zeros((8 - MOE_GROUPS, D_MODEL), F32), we.T], axis=0)
    br = jnp.concatenate([bg, jnp.zeros((8 - MOE_GROUPS,), F32), be]).reshape(ROUTE_ROWS, 1).astype(F32)
    ti = jnp.arange(tm)
    su = (ti[:, None] < ti[None, :]).astype(BF16)
    row = lambda i: (i, 0)
    fixed = lambda i: (0, 0)
    in_specs = [pl.BlockSpec((tm, D_MODEL), row)]
    in_specs += [pl.BlockSpec((tm, a.shape[1]), row) for a in acts]
    in_specs += [pl.BlockSpec(w.shape, fixed) for w in weights]
    in_specs += [pl.BlockSpec((1, D_MODEL), fixed), pl.BlockSpec(wr.shape, fixed), pl.BlockSpec(br.shape, fixed),
                 pl.BlockSpec(su.shape, fixed)]
    x1, hn, route, cnt = pl.pallas_call(
        functools.partial(_out_router_kernel, n_in=n_in),
        grid=(t // tm,),
        in_specs=in_specs,
        out_specs=[pl.BlockSpec((tm, D_MODEL), row), pl.BlockSpec((tm, D_MODEL), row),
                   pl.BlockSpec((8, tm), lambda i: (0, i)), pl.BlockSpec((MOE_EXPERTS, LANES), fixed)],
        out_shape=[jax.ShapeDtypeStruct((t, D_MODEL), F32), jax.ShapeDtypeStruct((t, D_MODEL), F32),
                   jax.ShapeDtypeStruct((8, t), F32), jax.ShapeDtypeStruct((MOE_EXPERTS, LANES), F32)],
        scratch_shapes=[pltpu.VMEM((MOE_EXPERTS, 1), F32)],
        compiler_params=_params("arbitrary"),
        name="out_proj_router",
    )(x2d, *acts, *weights, ffn_gain.reshape(1, D_MODEL), wr.astype(F32), br, su)
    return x1, hn, route, cnt[:, 0]


def _row_copy(src_ref, src_row, dst_ref, dst_row, sem):
    return pltpu.make_async_copy(src_ref.at[pl.ds(src_row, 1), :], dst_ref.at[pl.ds(dst_row, 1), :], sem)


def _dispatch_kernel(sa_ref, sb_ref, hn_ref, xs_in_ref, xs_ref, sem):
    del xs_in_ref
    tm = hn_ref.shape[0]

    def start(r, c):
        _row_copy(hn_ref, r, xs_ref, sa_ref[0, 0, r], sem).start()
        _row_copy(hn_ref, r, xs_ref, sb_ref[0, 0, r], sem).start()
        return c

    lax.fori_loop(0, tm, start, 0)

    def wait(r, c):
        _row_copy(hn_ref, r, xs_ref, sa_ref[0, 0, r], sem).wait()
        _row_copy(hn_ref, r, xs_ref, sb_ref[0, 0, r], sem).wait()
        return c

    lax.fori_loop(0, tm, wait, 0)


def _dispatch(hn, slot_a, slot_b, n_slots):
    t = hn.shape[0]
    tm = ROW_TILE
    smem = lambda: pl.BlockSpec((1, 1, tm), lambda i: (i, 0, 0), memory_space=pltpu.SMEM)
    xs0 = jnp.zeros((n_slots, D_MODEL), hn.dtype)
    return pl.pallas_call(
        _dispatch_kernel,
        grid=(t // tm,),
        in_specs=[smem(), smem(), pl.BlockSpec((tm, D_MODEL), lambda i: (i, 0)), pl.BlockSpec(memory_space=pl.ANY)],
        out_specs=pl.BlockSpec(memory_space=pl.ANY),
        out_shape=jax.ShapeDtypeStruct((n_slots, D_MODEL), hn.dtype),
        scratch_shapes=[pltpu.SemaphoreType.DMA(())],
        input_output_aliases={3: 0},
        compiler_params=_params("arbitrary"),
        name="moe_dispatch",
    )(slot_a.reshape(t // tm, 1, tm), slot_b.reshape(t // tm, 1, tm), hn, xs0)


def _ffn_kernel(te_ref, nv_ref, xs_ref, w1_ref, w3_ref, w2_ref, ys_ref):
    @pl.when(pl.program_id(0) < nv_ref[0])
    def _():
        x = xs_ref[...].astype(BF16)
        a = _dot(x, w1_ref[...])
        b = _dot(x, w3_ref[...])
        ys_ref[...] = _dot((_silu(a) * b).astype(BF16), w2_ref[...])

    @pl.when(pl.program_id(0) >= nv_ref[0])
    def _():
        ys_ref[...] = jnp.zeros_like(ys_ref)


def _expert_ffn(xs, tile_expert, n_valid, w1, w3, w2):
    n_slots = xs.shape[0]
    n_tiles = n_slots // EXPERT_TILE
    wmap = lambda i, te, nv: (te[i], 0, 0)
    grid_spec = pltpu.PrefetchScalarGridSpec(
        num_scalar_prefetch=2,
        grid=(n_tiles,),
        in_specs=[pl.BlockSpec((EXPERT_TILE, D_MODEL), lambda i, te, nv: (i, 0)),
                  pl.BlockSpec((None, D_MODEL, MOE_FF), wmap), pl.BlockSpec((None, D_MODEL, MOE_FF), wmap),
                  pl.BlockSpec((None, MOE_FF, D_MODEL), wmap)],
        out_specs=pl.BlockSpec((EXPERT_TILE, D_MODEL), lambda i, te, nv: (i, 0)),
    )
    return pl.pallas_call(
        _ffn_kernel,
        grid_spec=grid_spec,
        out_shape=jax.ShapeDtypeStruct((n_slots, D_MODEL), F32),
        compiler_params=_params("arbitrary"),
        name="moe_expert_ffn",
    )(tile_expert, n_valid, xs, w1, w3, w2)


def _combine_kernel(sa_ref, sb_ref, x_ref, wt_ref, ys_ref, o_ref, ya_ref, yb_ref, sem):
    tm = x_ref.shape[0]

    def start(r, c):
        _row_copy(ys_ref, sa_ref[0, 0, r], ya_ref, r, sem).start()
        _row_copy(ys_ref, sb_ref[0, 0, r], yb_ref, r, sem).start()
        return c

    lax.fori_loop(0, tm, start, 0)

    def wait(r, c):
        _row_copy(ys_ref, sa_ref[0, 0, r], ya_ref, r, sem).wait()
        _row_copy(ys_ref, sb_ref[0, 0, r], yb_ref, r, sem).wait()
        return c

    lax.fori_loop(0, tm, wait, 0)
    wt = wt_ref[...]
    o_ref[...] = x_ref[...] + wt[:, 2:3] * ya_ref[...] + wt[:, 3:4] * yb_ref[...]


def _combine(x1, ys, slot_a, slot_b, wts):
    t = x1.shape[0]
    tm = ROW_TILE
    smem = lambda: pl.BlockSpec((1, 1, tm), lambda i: (i, 0, 0), memory_space=pltpu.SMEM)
    row = lambda i: (i, 0)
    return pl.pallas_call(
        _combine_kernel,
        grid=(t // tm,),
        in_specs=[smem(), smem(), pl.BlockSpec((tm, D_MODEL), row), pl.BlockSpec((tm, 8), row),
                  pl.BlockSpec(memory_space=pl.ANY)],
        out_specs=pl.BlockSpec((tm, D_MODEL), row),
        out_shape=jax.ShapeDtypeStruct((t, D_MODEL), F32),
        scratch_shapes=[pltpu.VMEM((tm, D_MODEL), F32), pltpu.VMEM((tm, D_MODEL), F32),
                        pltpu.SemaphoreType.DMA(())],
        compiler_params=_params("arbitrary"),
        name="moe_combine",
    )(slot_a.reshape(t // tm, 1, tm), slot_b.reshape(t // tm, 1, tm), x1, wts, ys)


def _moe(x1, hn, route, counts, w1, w3, w2):
    t = x1.shape[0]
    n_tiles = (2 * t) // EXPERT_TILE + MOE_EXPERTS
    n_slots = n_tiles * EXPERT_TILE
    counts = counts.astype(jnp.int32)
    tiles_per = (counts + EXPERT_TILE - 1) // EXPERT_TILE
    tile_end = jnp.cumsum(tiles_per)
    seg_start = (tile_end - tiles_per) * EXPERT_TILE
    n_valid = tile_end[-1:]
    tile_expert = jnp.minimum(jnp.searchsorted(tile_end, jnp.arange(n_tiles, dtype=jnp.int32), side="right"),
                              MOE_EXPERTS - 1).astype(jnp.int32)
    ri = route.astype(jnp.int32)
    slot_a = seg_start[ri[0]] + ri[4]
    slot_b = seg_start[ri[1]] + ri[5]
    xs = _dispatch(hn, slot_a, slot_b, n_slots)
    ys = _expert_ffn(xs, tile_expert, n_valid, w1.astype(BF16), w3.astype(BF16), w2.astype(BF16))
    return _combine(x1, ys, slot_a, slot_b, jnp.transpose(route))


def _odd_proj_kernel(x_ref, g_ref, wz_ref, wx_ref, wd_ref, z_ref, xbc_ref, dt_ref):
    xn = _rms_rows(x_ref[...], g_ref[...]).astype(BF16)
    z_ref[...] = _dot(xn, wz_ref[...])
    xbc_ref[...] = _dot(xn, wx_ref[...])
    dt_ref[...] = _dot(xn, wd_ref[...])


def _odd_proj(x2d, gain, w_in):
    t = x2d.shape[0]
    tm = ROW_TILE
    wz = w_in[:, :SSM_INNER].astype(BF16)
    wx = w_in[:, SSM_INNER:SSM_INNER + SSM_CONV_CH].astype(BF16)
    wd = jnp.pad(w_in[:, SSM_INNER + SSM_CONV_CH:], ((0, 0), (0, LANES - SSM_HEADS))).astype(BF16)
    row = lambda i: (i, 0)
    fixed = lambda i: (0, 0)
    return pl.pallas_call(
        _odd_proj_kernel,
        grid=(t // tm,),
        in_specs=[pl.BlockSpec((tm, D_MODEL), row), pl.BlockSpec((1, D_MODEL), fixed),
                  pl.BlockSpec(wz.shape, fixed), pl.BlockSpec(wx.shape, fixed), pl.BlockSpec(wd.shape, fixed)],
        out_specs=[pl.BlockSpec((tm, SSM_INNER), row), pl.BlockSpec((tm, SSM_CONV_CH), row),
                   pl.BlockSpec((tm, LANES), row)],
        out_shape=[jax.ShapeDtypeStruct((t, SSM_INNER), F32), jax.ShapeDtypeStruct((t, SSM_CONV_CH), F32),
                   jax.ShapeDtypeStruct((t, LANES), F32)],
        compiler_params=_params("parallel"),
        name="odd_proj",
    )(x2d, gain.reshape(1, D_MODEL), wz, wx, wd)


def _ssd_kernel(z_ref, xbc_ref, dt_ref, cw_ref, cb_ref, dtb_ref, a_ref, dskip_ref, ng_ref, tri_ref, exp_ref,
                y_ref, st_ref, ext_ref, xc_ref):
    rows_blk = xbc_ref.shape[0]

    @pl.when(pl.program_id(1) == 0)
    def _():
        st_ref[...] = jnp.zeros_like(st_ref)
        ext_ref[0:SUBLANES, :] = jnp.zeros((SUBLANES, SSM_CONV_CH), F32)

    cur = xbc_ref[...]
    ext_ref[SUBLANES:, :] = cur
    acc = cb_ref[...] + cw_ref[SSM_CONV - 1:SSM_CONV, :] * cur
    for k in range(1, SSM_CONV):
        acc = acc + cw_ref[SSM_CONV - 1 - k:SSM_CONV - k, :] * ext_ref[SUBLANES - k:SUBLANES - k + rows_blk, :]
    ext_ref[0:SUBLANES, :] = cur[rows_blk - SUBLANES:]
    xc_ref[...] = _silu(acc)

    tri = tri_ref[...]
    expand = exp_ref[...]
    a_neg = -jnp.exp(a_ref[...])
    lane = lax.broadcasted_iota(jnp.int32, (CHUNK, LANES), 1)
    trow = lax.broadcasted_iota(jnp.int32, (CHUNK, LANES), 0)
    left = lane < SSM_HEAD_DIM
    diag = trow == jnp.where(left, lane, lane - SSM_HEAD_DIM)
    causal = trow >= jnp.where(left, lane, lane - SSM_HEAD_DIM)
    heads_per_group = SSM_HEADS // SSM_GROUPS
    b_off = SSM_INNER
    c_off = SSM_INNER + SSM_GROUPS * SSM_STATE

    def chunk(c, carry):
        r0 = pl.multiple_of(c * CHUNK, CHUNK)
        rows = pl.ds(r0, CHUNK)
        dt = dt_ref[rows, :] + dtb_ref[...]
        dt = jnp.maximum(dt, 0.0) + jnp.log(1.0 + jnp.exp(-jnp.abs(dt)))
        cum = _dot_f32(_dot_f32(tri, dt * a_neg), expand)
        dtx = _dot_f32(dt, expand)
        xs = xc_ref[rows, 0:SSM_INNER]
        xdt = xs * dtx
        cum_last = cum[CHUNK - 1:CHUNK, :]
        decay_in = jnp.exp(cum)
        decay_out = jnp.exp(cum_last - cum)
        xw = (xdt * decay_out).astype(BF16)
        y_parts = []
        for g in range(SSM_GROUPS):
            gl = slice(g * SSM_GROUP_WIDTH, (g + 1) * SSM_GROUP_WIDTH)
            bm = xc_ref[rows, b_off + g * SSM_STATE:b_off + (g + 1) * SSM_STATE].astype(BF16)
            cm = xc_ref[rows, c_off + g * SSM_STATE:c_off + (g + 1) * SSM_STATE].astype(BF16)
            st = st_ref[g]
            y_g = _dot(cm, st.astype(BF16)) * decay_in[:, gl]
            st_ref[g] = st * jnp.exp(cum_last[:, gl]) + _dot_tn(bm, xw[:, gl])
            cb2 = _dot_nt(cm, jnp.concatenate([bm, bm], axis=0))
            pair_out = []
            for pr in range(heads_per_group // 2):
                sl = slice(g * SSM_GROUP_WIDTH + pr * LANES, g * SSM_GROUP_WIDTH + (pr + 1) * LANES)
                cum_p = cum[:, sl]
                cum_row = jnp.sum(jnp.where(diag, cum_p, 0.0), axis=0, keepdims=True)
                lmat = jnp.where(causal, jnp.exp(jnp.minimum(cum_p - cum_row, 0.0)), 0.0)
                xp = xdt[:, sl]
                x2 = jnp.concatenate([jnp.where(left, xp, 0.0), jnp.where(left, 0.0, xp)], axis=0).astype(BF16)
                pair_out.append(_dot((cb2 * lmat).astype(BF16), x2))
            y_parts.append(y_g + jnp.concatenate(pair_out, axis=1))
        y = jnp.concatenate(y_parts, axis=1) + dskip_ref[...] * xs
        y = y * _silu(z_ref[rows, :])
        outs = []
        for g in range(SSM_GROUPS):
            gl = slice(g * SSM_GROUP_WIDTH, (g + 1) * SSM_GROUP_WIDTH)
            outs.append(_rms_rows(y[:, gl], ng_ref[:, gl]))
        y_ref[rows, :] = jnp.concatenate(outs, axis=1).astype(y_ref.dtype)
        return carry

    lax.fori_loop(0, rows_blk // CHUNK, chunk, 0)


def _ssd(z, xbc, dt, conv_w, conv_b, dt_bias, a_log, d_skip, norm_g, batch, seq):
    t = z.shape[0]
    rb = SSD_BLOCK
    nblk = seq // rb
    tri = jnp.tril(jnp.ones((CHUNK, CHUNK), F32))
    lane = jnp.arange(SSM_INNER)
    expand = (jnp.arange(LANES)[:, None] == (lane[None, :] // SSM_HEAD_DIM)).astype(F32)
    pad_h = lambda v: jnp.pad(v.astype(F32), (0, LANES - SSM_HEADS)).reshape(1, LANES)
    row = lambda b, s: (b * nblk + s, 0)
    fixed = lambda b, s: (0, 0)
    return pl.pallas_call(
        _ssd_kernel,
        grid=(batch, nblk),
        in_specs=[pl.BlockSpec((rb, SSM_INNER), row), pl.BlockSpec((rb, SSM_CONV_CH), row),
                  pl.BlockSpec((rb, LANES), row),
                  pl.BlockSpec((SSM_CONV, SSM_CONV_CH), fixed), pl.BlockSpec((1, SSM_CONV_CH), fixed),
                  pl.BlockSpec((1, LANES), fixed), pl.BlockSpec((1, LANES), fixed),
                  pl.BlockSpec((1, SSM_INNER), fixed), pl.BlockSpec((1, SSM_INNER), fixed),
                  pl.BlockSpec((CHUNK, CHUNK), fixed), pl.BlockSpec((LANES, SSM_INNER), fixed)],
        out_specs=pl.BlockSpec((rb, SSM_INNER), row),
        out_shape=jax.ShapeDtypeStruct((t, SSM_INNER), BF16),
        scratch_shapes=[pltpu.VMEM((SSM_GROUPS, SSM_STATE, SSM_GROUP_WIDTH), F32),
                        pltpu.VMEM((rb + SUBLANES, SSM_CONV_CH), F32), pltpu.VMEM((rb, SSM_CONV_CH), F32)],
        compiler_params=_params("parallel", "arbitrary"),
        name="ssd_scan",
    )(z, xbc, dt, conv_w.astype(F32), conv_b.reshape(1, SSM_CONV_CH).astype(F32), pad_h(dt_bias), pad_h(a_log),
      jnp.repeat(d_skip.astype(F32), SSM_HEAD_DIM).reshape(1, SSM_INNER), norm_g.reshape(1, SSM_INNER).astype(F32),
      tri, expand)


def kernel(x, mix_norm, ffn_norm, ev_w_in, hg_lb_logits, hg_out_norm, at_q_norm, at_k_norm, ev_w_out, od_w_in,
           od_conv_w, od_conv_b, od_dt_bias, od_A_log, od_D, od_out_norm, od_w_out, moe_wg, moe_bg, moe_we,
           moe_be, moe_w1, moe_w3, moe_w2):
    batch, seq, d = x.shape
    assert d == D_MODEL and seq % HG_BLOCK == 0 and seq % KEY_TILE == 0 and (batch * seq) % ROW_TILE == 0
    depth = mix_norm.shape[0]
    x2d = x.reshape(batch * seq, d).astype(F32)
    lbs = jnp.cumsum(jax.nn.softmax(hg_lb_logits.astype(F32), axis=0), axis=0)
    for layer in range(depth):
        i = layer // 2
        if layer % 2 == 0:
            hproj, q, kv, iq, ik, iw = _even_proj(x2d, mix_norm[layer], ev_w_in[i], at_q_norm[i], at_k_norm[i], seq)
            o_h = _hgrn(hproj, lbs[i], hg_out_norm[i], batch, seq)
            o_a = _dsa(q, iq, iw, kv, ik, batch, seq)
            w_h = ev_w_out[i][:HG_WIDTH].astype(BF16)
            w_a = jnp.pad(ev_w_out[i][HG_WIDTH:].reshape(AT_HEADS, AT_DIM, d),
                          ((0, 0), (0, LANES - AT_DIM), (0, 0))).reshape(AT_HEADS * LANES, d).astype(BF16)
            acts, weights = [o_h, o_a], [w_h, w_a]
        else:
            z, xbc, dt = _odd_proj(x2d, mix_norm[layer], od_w_in[i])
            y = _ssd(z, xbc, dt, od_conv_w[i], od_conv_b[i], od_dt_bias[i], od_A_log[i], od_D[i], od_out_norm[i],
                     batch, seq)
            acts, weights = [y], [od_w_out[i].astype(BF16)]
        x1, hn, route, counts = _out_router(x2d, acts, weights, ffn_norm[layer], moe_wg[layer], moe_bg[layer],
                                            moe_we[layer], moe_be[layer])
        x2d = _moe(x1, hn, route, counts, moe_w1[layer], moe_w3[layer], moe_w2[layer])
    return x2d.reshape(batch, seq, d).astype(x.dtype)
```

```python
import functools
import math

import jax
import jax.numpy as jnp
from jax import lax
from jax.experimental import pallas as pl
from jax.experimental.pallas import tpu as pltpu

D_MODEL = 1024
CHUNK = 64
Q_BLOCK = 128
ROPE_THETA = 10000.0
EPS = 1e-6

HG_HEADS = 4
HG_DIM = 128
HG_WIDTH = HG_HEADS * HG_DIM
AT_HEADS = 8
AT_KV_HEADS = 2
AT_GROUP = AT_HEADS // AT_KV_HEADS
AT_DIM = 64
AT_WIDTH = AT_HEADS * AT_DIM
KV_WIDTH = AT_KV_HEADS * AT_DIM
IDX_HEADS = 4
IDX_DIM = 64
TOPK_MAX = 256

SSM_INNER = 2 * D_MODEL
SSM_HEAD_DIM = 64
SSM_HEADS = SSM_INNER // SSM_HEAD_DIM
SSM_GROUPS = 4
SSM_STATE = 128
SSM_CONV = 4
SSM_CONV_CH = SSM_INNER + 2 * SSM_GROUPS * SSM_STATE
SSM_GROUP_WIDTH = SSM_INNER // SSM_GROUPS

MOE_GROUPS = 4
MOE_PER_GROUP = 8
MOE_EXPERTS = MOE_GROUPS * MOE_PER_GROUP
MOE_FF = 512

LANES = 128
SUBLANES = 8
VMEM_LIMIT = 56 * 1024 * 1024
ROW_TILE = 256
ROUTER_TILE = 512
KEY_TILE = 512
HG_BLOCK = 512
SSD_BLOCK = 256
EXPERT_TILE = 512
ROUTE_ROWS = 8 + MOE_EXPERTS
INT_MIN = -(2 ** 31)
NEG_BIG = -1e30

F32 = jnp.float32
BF16 = jnp.bfloat16
HIGHEST = lax.Precision.HIGHEST


def _params(*sem):
    return pltpu.CompilerParams(dimension_semantics=sem, vmem_limit_bytes=VMEM_LIMIT)


def _dot(a, b):
    return jnp.dot(a, b, preferred_element_type=F32)


def _dot_f32(a, b):
    return jnp.dot(a, b, preferred_element_type=F32, precision=HIGHEST)


def _dot_nt(a, b, precision=None):
    return lax.dot_general(a, b, (((1,), (1,)), ((), ())), preferred_element_type=F32, precision=precision)


def _dot_tn(a, b, precision=None):
    return lax.dot_general(a, b, (((0,), (0,)), ((), ())), preferred_element_type=F32, precision=precision)


def _sigmoid(x):
    return 1.0 / (1.0 + jnp.exp(-x))


def _silu(x):
    return x * _sigmoid(x)


def _rms_rows(x, gain):
    return x * lax.rsqrt(jnp.mean(x * x, axis=-1, keepdims=True) + EPS) * gain


def _rope_slab(x, cos, sin_lo, sin_hi):
    return x * cos + pltpu.roll(x, LANES - 32, 1) * sin_lo + pltpu.roll(x, 32, 1) * sin_hi


def _even_proj_kernel(x_ref, g_ref, wh_ref, wq_ref, wkv_ref, wiq_ref, wik_ref,
                      r1_ref, gm_ref, qn_ref, kn_ref,
                      h_ref, q_ref, k_ref, v_ref, iq_ref, ik_ref, iw_ref):
    xn = _rms_rows(x_ref[...], g_ref[...]).astype(BF16)
    h_ref[...] = _dot(xn, wh_ref[...])

    cos1, lo1, hi1 = r1_ref[0], r1_ref[1], r1_ref[2]
    gm = gm_ref[...]

    def qk_norm(v, gain):
        ms = _dot_f32(v * v, gm)
        return v * lax.rsqrt(ms + EPS) * gain

    q = _dot(xn, wq_ref[...])
    for h in range(AT_HEADS):
        sl = slice(h * LANES, (h + 1) * LANES)
        qh = _rope_slab(qk_norm(q[:, sl], qn_ref[...]), cos1, lo1, hi1)
        q_ref[:, sl] = (qh * (AT_DIM ** -0.5)).astype(BF16)
    kv = _dot(xn, wkv_ref[...])
    for h in range(AT_KV_HEADS):
        sl = slice(h * LANES, (h + 1) * LANES)
        k_ref[:, sl] = _rope_slab(qk_norm(kv[:, sl], kn_ref[...]), cos1, lo1, hi1).astype(BF16)
    v = kv[:, AT_KV_HEADS * LANES:]
    lane = lax.broadcasted_iota(jnp.int32, v.shape, 1)
    v_ref[...] = jnp.where(lane % LANES == AT_DIM, 1.0, v).astype(BF16)
    iq = _dot(xn, wiq_ref[...])
    for h in range(IDX_HEADS):
        sl = slice(h * LANES, (h + 1) * LANES)
        iq_ref[:, sl] = (_rope_slab(iq[:, sl], cos1, lo1, hi1) * (IDX_DIM ** -0.5)).astype(BF16)
    ikw = _rope_slab(_dot(xn, wik_ref[...]), cos1, lo1, hi1)
    ik_ref[...] = ikw.astype(BF16)
    iw_ref[...] = ikw


def _pad_heads(w, heads):
    d = w.shape[0]
    w = w.reshape(d, heads, 64)
    return jnp.pad(w, ((0, 0), (0, 0), (0, 64))).reshape(d, heads * LANES)


def _rope_tables(seq):
    half = AT_DIM // 2
    inv = jnp.exp(-math.log(ROPE_THETA) * jnp.arange(half, dtype=F32) / half)
    ang = jnp.arange(seq, dtype=F32)[:, None] * inv[None, :]
    c, s = jnp.cos(ang), jnp.sin(ang)
    z, o = jnp.zeros_like(c), jnp.ones_like(c)
    return jnp.stack([jnp.concatenate([c, c, o, o], 1), jnp.concatenate([-s, z, z, z], 1),
                      jnp.concatenate([z, s, z, z], 1)])


def _even_proj(x2d, gain, w_in, q_norm, k_norm, seq):
    t = x2d.shape[0]
    tm = ROW_TILE
    o = [0, 4 * HG_WIDTH]
    for wdt in (AT_WIDTH, KV_WIDTH, KV_WIDTH, IDX_HEADS * IDX_DIM, IDX_DIM, IDX_HEADS):
        o.append(o[-1] + wdt)
    wh = w_in[:, o[0]:o[1]].astype(BF16)
    wq = _pad_heads(w_in[:, o[1]:o[2]], AT_HEADS).astype(BF16)
    wkv = jnp.concatenate([_pad_heads(w_in[:, o[2]:o[3]], AT_KV_HEADS),
                           _pad_heads(w_in[:, o[3]:o[4]], AT_KV_HEADS)], 1).astype(BF16)
    wiq = _pad_heads(w_in[:, o[4]:o[5]], IDX_HEADS).astype(BF16)
    wik = jnp.pad(w_in[:, o[5]:o[7]], ((0, 0), (0, LANES - IDX_DIM - IDX_HEADS))).astype(BF16)
    rope1 = _rope_tables(seq)
    lane = jnp.arange(LANES)
    gm = jnp.where((lane[:, None] < AT_DIM) & (lane[None, :] < AT_DIM), 1.0 / AT_DIM, 0.0).astype(F32)
    pad_gain = lambda g: jnp.pad(g, (0, LANES - AT_DIM), constant_values=1.0).reshape(1, LANES)
    nseq = seq // tm
    row = lambda i: (i, 0)
    fixed = lambda i: (0, 0)
    rope_map = lambda i: (0, i % nseq, 0)
    outs = pl.pallas_call(
        _even_proj_kernel,
        grid=(t // tm,),
        in_specs=[pl.BlockSpec((tm, D_MODEL), row), pl.BlockSpec((1, D_MODEL), fixed),
                  pl.BlockSpec(wh.shape, fixed), pl.BlockSpec(wq.shape, fixed), pl.BlockSpec(wkv.shape, fixed),
                  pl.BlockSpec(wiq.shape, fixed), pl.BlockSpec(wik.shape, fixed),
                  pl.BlockSpec((3, tm, LANES), rope_map),
                  pl.BlockSpec((LANES, LANES), fixed), pl.BlockSpec((1, LANES), fixed),
                  pl.BlockSpec((1, LANES), fixed)],
        out_specs=[pl.BlockSpec((tm, 4 * HG_WIDTH), row), pl.BlockSpec((tm, AT_HEADS * LANES), row),
                   pl.BlockSpec((tm, AT_KV_HEADS * LANES), row), pl.BlockSpec((tm, AT_KV_HEADS * LANES), row),
                   pl.BlockSpec((tm, IDX_HEADS * LANES), row),
                   pl.BlockSpec((tm, LANES), row), pl.BlockSpec((tm, LANES), row)],
        out_shape=[jax.ShapeDtypeStruct((t, 4 * HG_WIDTH), F32), jax.ShapeDtypeStruct((t, AT_HEADS * LANES), BF16),
                   jax.ShapeDtypeStruct((t, AT_KV_HEADS * LANES), BF16),
                   jax.ShapeDtypeStruct((t, AT_KV_HEADS * LANES), BF16),
                   jax.ShapeDtypeStruct((t, IDX_HEADS * LANES), BF16),
                   jax.ShapeDtypeStruct((t, LANES), BF16), jax.ShapeDtypeStruct((t, LANES), F32)],
        compiler_params=_params("parallel"),
        name="even_proj",
    )(x2d, gain.reshape(1, D_MODEL), wh, wq, wkv, wiq, wik, rope1, gm, pad_gain(q_norm), pad_gain(k_norm))
    return outs


HG_SUB = 16


def _hgrn_kernel(h_ref, lb_ref, gn_ref, tri_ref, o_ref, st_ref, od_ref):
    @pl.when(pl.program_id(1) == 0)
    def _():
        st_ref[...] = jnp.zeros_like(st_ref)

    tri = tri_ref[...]
    s_idx = lax.broadcasted_iota(jnp.int32, (HG_SUB, 1), 0)

    def chunk(c, carry):
        r0 = pl.multiple_of(c * CHUNK, CHUNK)
        rows = pl.ds(r0, CHUNK)
        for h in range(HG_HEADS):
            head_chunk(rows, h)
        return carry

    def head_chunk(rows, h):
        sl = slice(h * HG_DIM, (h + 1) * HG_DIM)
        col = lambda stream: slice((stream * HG_HEADS + h) * HG_DIM, (stream * HG_HEADS + h + 1) * HG_DIM)
        lb = lb_ref[:, sl]
        od = od_ref.at[h]
        q = h_ref[rows, col(0)]
        hf = h_ref[rows, col(1)]
        v = h_ref[rows, col(2)]
        f = lb + (1.0 - lb) * _sigmoid(hf)
        kk = (1.0 - lb) * _sigmoid(-hf)
        b = _dot_f32(tri, jnp.log(f))
        st = st_ref[h]
        o = _dot_nt((q * jnp.exp(b)).astype(BF16), st.astype(BF16))
        off = []
        for blk in range(CHUNK // HG_SUB):
            lo, hi = blk * HG_SUB, (blk + 1) * HG_SUB
            bs, ks, vs = b[lo:hi], kk[lo:hi], v[lo:hi]
            for tt in range(HG_SUB):
                t = lo + tt
                e = jnp.exp(jnp.minimum(b[t:t + 1] - bs, 0.0))
                a = jnp.sum(e * (q[t:t + 1] * ks), axis=-1, keepdims=True)
                a = jnp.where(s_idx <= tt, a, 0.0)
                od[t:t + 1, :] = jnp.sum(a * vs, axis=0, keepdims=True)
            if blk == 0:
                off.append(jnp.zeros((HG_SUB, HG_DIM), F32))
            else:
                ref = b[lo - 1:lo]
                qd = (q[lo:hi] * jnp.exp(b[lo:hi] - ref)).astype(BF16)
                kd = (kk[:lo] * jnp.exp(ref - b[:lo])).astype(BF16)
                off.append(_dot(_dot_nt(qd, kd).astype(BF16), v[:lo].astype(BF16)))
        o = o + od[...] + jnp.concatenate(off, axis=0)
        b_last = b[CHUNK - 1:CHUNK, :]
        st_ref[h] = st * jnp.exp(b_last) + _dot_tn(v.astype(BF16), (kk * jnp.exp(b_last - b)).astype(BF16))
        o = _rms_rows(o, gn_ref[:, sl]) * _silu(h_ref[rows, col(3)])
        o_ref[rows, sl] = o.astype(o_ref.dtype)

    lax.fori_loop(0, HG_BLOCK // CHUNK, chunk, 0)


def _hgrn(hproj, lb, out_norm, batch, seq):
    t = hproj.shape[0]
    nblk = seq // HG_BLOCK
    tri = jnp.tril(jnp.ones((CHUNK, CHUNK), F32))
    row = lambda b, s: (b * nblk + s, 0)
    fixed = lambda b, s: (0, 0)
    return pl.pallas_call(
        _hgrn_kernel,
        grid=(batch, nblk),
        in_specs=[pl.BlockSpec((HG_BLOCK, 4 * HG_WIDTH), row), pl.BlockSpec((1, HG_WIDTH), fixed),
                  pl.BlockSpec((1, HG_WIDTH), fixed), pl.BlockSpec((CHUNK, CHUNK), fixed)],
        out_specs=pl.BlockSpec((HG_BLOCK, HG_WIDTH), row),
        out_shape=jax.ShapeDtypeStruct((t, HG_WIDTH), BF16),
        scratch_shapes=[pltpu.VMEM((HG_HEADS, HG_DIM, HG_DIM), F32), pltpu.VMEM((HG_HEADS, CHUNK, HG_DIM), F32)],
        compiler_params=_params("parallel", "arbitrary"),
        name="hgrn2_scan",
    )(hproj, lb.reshape(1, HG_WIDTH), out_norm.reshape(1, HG_WIDTH), tri)


COUNT_LANES = 4
SEARCH_UNTESTED_BITS = 18


def _dsa_kernel(q_ref, iq_ref, iw_ref, k_ref, vt_ref, ik_ref, tril_ref, o_ref, keys_ref, *, topk):
    j = pl.program_id(1)
    kb = KEY_TILE
    nkt = (j * Q_BLOCK + Q_BLOCK + kb - 1) // kb
    qrow = lax.broadcasted_iota(jnp.int32, (1, Q_BLOCK), 1)
    nadm = ((j * Q_BLOCK + qrow) // CHUNK + 1) * CHUNK
    key_row = lax.broadcasted_iota(jnp.int32, (kb, Q_BLOCK), 0)
    iwt = jnp.transpose(iw_ref[...])
    iws = [iwt[IDX_DIM + h:IDX_DIM + h + 1, :] for h in range(IDX_HEADS)]

    def score_tile(kt, carry):
        ik = ik_ref[kt]
        score = jnp.zeros((kb, Q_BLOCK), F32)
        for h in range(IDX_HEADS):
            sc = _dot_nt(ik, iq_ref[:, h * LANES:(h + 1) * LANES])
            score = score + iws[h] * jnp.maximum(sc, 0.0)
        bits = pltpu.bitcast(score, jnp.int32)
        key = jnp.where(bits < 0, bits ^ 0x7FFFFFFF, bits)
        key = jnp.where(bits == INT_MIN, 0, key)
        keys_ref[kt] = jnp.where(kt * kb + key_row < nadm, key, INT_MIN)
        return carry

    lax.fori_loop(0, nkt, score_tile, 0)

    @pl.when(nkt % 2 == 1)
    def _():
        keys_ref[nkt] = jnp.full((kb, Q_BLOCK), INT_MIN, jnp.int32)

    def fold(ind):
        return jnp.sum(ind.reshape(COUNT_LANES, kb // (COUNT_LANES * SUBLANES), SUBLANES, Q_BLOCK), axis=1)

    def count(indicator):
        def body(p, acc):
            return acc + fold(indicator(keys_ref[2 * p])) + fold(indicator(keys_ref[2 * p + 1]))
        acc = lax.fori_loop(0, (nkt + 1) // 2, body, jnp.zeros((COUNT_LANES, SUBLANES, Q_BLOCK), F32))
        return jnp.sum(jnp.sum(acc, axis=0), axis=0, keepdims=True)

    zero_ge = count(lambda ks: jnp.where(ks >= 0, 1.0, 0.0))
    zero_gt = count(lambda ks: jnp.where(ks > 0, 1.0, 0.0))
    at_zero = jnp.where(zero_gt < topk, jnp.where(zero_ge >= topk, 1.0, 0.0), 0.0)
    settled0 = jnp.where(nadm < topk, 1.0, at_zero)
    found0 = jnp.where(nadm < topk, INT_MIN, 0)

    def search_bit(i, st):
        tu, found, settled = st
        cand = tu | jnp.left_shift(jnp.int32(1), 31 - i)
        cand_s = cand ^ INT_MIN
        cnt = count(lambda ks: jnp.where(ks >= cand_s, 1.0, 0.0))
        hit = jnp.where(settled > 0.5, 0.0, jnp.where(cnt == topk, 1.0, 0.0))
        found = jnp.where(hit > 0.5, cand_s, found)
        settled = jnp.maximum(settled, hit)
        tu = jnp.where(cnt >= topk, cand, tu)
        return tu, found, settled

    st = lax.fori_loop(0, SEARCH_UNTESTED_BITS, search_bit, (jnp.zeros((1, Q_BLOCK), jnp.int32), found0, settled0))

    def search_cond(c):
        i, _, open_rows = c
        return jnp.logical_and(i < 32, open_rows > 0.5)

    def search_step(c):
        i, st, _ = c
        st = search_bit(i, st)
        return i + 1, st, jnp.max(1.0 - st[2])

    _, (tu, found, settled), _ = lax.while_loop(
        search_cond, search_step, (jnp.int32(SEARCH_UNTESTED_BITS), st, jnp.max(1.0 - st[2])))
    thr = jnp.where(settled > 0.5, found, tu ^ INT_MIN)
    need = topk - count(lambda ks: jnp.where(ks > thr, 1.0, 0.0))

    rows4 = AT_GROUP * Q_BLOCK
    qs = [jnp.concatenate([q_ref[:, (g * AT_GROUP + hh) * LANES:(g * AT_GROUP + hh + 1) * LANES]
                           for hh in range(AT_GROUP)], axis=0) for g in range(AT_KV_HEADS)]

    def masked_scores(kt, seen):
        ks = keys_ref[kt]
        tied = jnp.where(ks == thr, 1.0, 0.0)
        rank = _dot(tril_ref[...], tied.astype(BF16)) + seen
        tie = jnp.where(ks == thr, jnp.where(rank <= need, 0.0, NEG_BIG), NEG_BIG)
        bias = jnp.where(kt * kb + key_row < nadm, jnp.where(ks > thr, 0.0, tie), NEG_BIG)
        bias4 = jnp.concatenate([bias] * AT_GROUP, axis=1)
        k_tile = k_ref[kt]
        return rank[kb - 1:kb, :], tuple(_dot_nt(k_tile[:, g * LANES:(g + 1) * LANES], qs[g]) + bias4
                                         for g in range(AT_KV_HEADS))

    def att_tile(kt, carry):
        seen, state = carry
        seen, scores = masked_scores(kt, seen)
        vt_tile = vt_ref[kt]
        out = []
        for g in range(AT_KV_HEADS):
            m, acc = state[g]
            s = scores[g]
            m_new = jnp.maximum(m, jnp.max(s, axis=0, keepdims=True))
            alpha = jnp.exp(m - m_new)
            p = jnp.exp(s - m_new).astype(BF16)
            acc = alpha * acc + _dot(vt_tile[g * LANES:(g + 1) * LANES, :], p)
            out.append((m_new, acc))
        return seen, tuple(out)

    init = tuple((jnp.full((1, rows4), 0.1 * NEG_BIG, F32), jnp.zeros((LANES, rows4), F32))
                 for _ in range(AT_KV_HEADS))
    _, res = lax.fori_loop(0, nkt, att_tile, (jnp.zeros((1, Q_BLOCK), F32), init))
    for g in range(AT_KV_HEADS):
        _, acc = res[g]
        og = acc / acc[AT_DIM:AT_DIM + 1, :]
        for hh in range(AT_GROUP):
            h = g * AT_GROUP + hh
            o_ref[:, h * LANES:(h + 1) * LANES] = jnp.transpose(og[:, hh * Q_BLOCK:(hh + 1) * Q_BLOCK]).astype(o_ref.dtype)


def _dsa(q, iq, iw, k, v, ik, batch, seq):
    t = q.shape[0]
    nb = seq // Q_BLOCK
    nkt = seq // KEY_TILE
    topk = min(TOPK_MAX, seq // 4)
    width = AT_KV_HEADS * LANES
    k3 = k.reshape(batch * nkt, KEY_TILE, width)
    vt3 = jnp.swapaxes(v.reshape(batch * nkt, KEY_TILE, width), 1, 2)
    ik3 = ik.reshape(batch * nkt, KEY_TILE, LANES)
    ki = jnp.arange(KEY_TILE)
    tril = (ki[None, :] <= ki[:, None]).astype(BF16)
    qrow = lambda b, j: (b * nb + j, 0)
    seq_blk = lambda b, j: (b, 0, 0)
    return pl.pallas_call(
        functools.partial(_dsa_kernel, topk=topk),
        grid=(batch, nb),
        in_specs=[pl.BlockSpec((Q_BLOCK, AT_HEADS * LANES), qrow), pl.BlockSpec((Q_BLOCK, IDX_HEADS * LANES), qrow),
                  pl.BlockSpec((Q_BLOCK, LANES), qrow),
                  pl.BlockSpec((nkt, KEY_TILE, width), seq_blk), pl.BlockSpec((nkt, width, KEY_TILE), seq_blk),
                  pl.BlockSpec((nkt, KEY_TILE, LANES), seq_blk),
                  pl.BlockSpec((KEY_TILE, KEY_TILE), lambda b, j: (0, 0))],
        out_specs=pl.BlockSpec((Q_BLOCK, AT_HEADS * LANES), qrow),
        out_shape=jax.ShapeDtypeStruct((t, AT_HEADS * LANES), BF16),
        scratch_shapes=[pltpu.VMEM((nkt + 1, KEY_TILE, Q_BLOCK), jnp.int32)],
        compiler_params=_params("parallel", "arbitrary"),
        name="dsa_attention",
    )(q, iq, iw, k3, vt3, ik3, tril)


def _out_router_kernel(*refs, n_in):
    x_ref = refs[0]
    a_refs = refs[1:1 + n_in]
    w_refs = refs[1 + n_in:1 + 2 * n_in]
    g_ref, wr_ref, br_ref, su_ref = refs[1 + 2 * n_in:5 + 2 * n_in]
    x1_ref, hn_ref, route_ref, cnt_ref = refs[5 + 2 * n_in:9 + 2 * n_in]
    carry_ref = refs[9 + 2 * n_in]

    @pl.when(pl.program_id(0) == 0)
    def _():
        carry_ref[...] = jnp.zeros_like(carry_ref)

    x1 = x_ref[...]
    for a_ref, w_ref in zip(a_refs, w_refs):
        x1 = x1 + _dot(a_ref[...], w_ref[...])
    x1_ref[...] = x1
    hn = _rms_rows(x1, g_ref[...])
    hn_ref[...] = hn

    lt = _dot_nt(wr_ref[...], hn, precision=HIGHEST) + br_ref[...]
    tm = lt.shape[1]
    r4 = lax.broadcasted_iota(jnp.int32, (MOE_GROUPS, tm), 0).astype(F32)
    gl = lt[0:MOE_GROUPS]
    gmax = jnp.max(gl, axis=0, keepdims=True)
    gidx = jnp.min(jnp.where(gl == gmax, r4, float(MOE_GROUPS)), axis=0, keepdims=True)
    gval = 1.0 / jnp.sum(jnp.exp(gl - gmax), axis=0, keepdims=True)
    el = jnp.zeros((MOE_PER_GROUP, tm), F32)
    for g in range(MOE_GROUPS):
        el = el + jnp.where(gidx == float(g), lt[8 + g * MOE_PER_GROUP:8 + (g + 1) * MOE_PER_GROUP], 0.0)
    r8 = lax.broadcasted_iota(jnp.int32, (MOE_PER_GROUP, tm), 0).astype(F32)
    m1 = jnp.max(el, axis=0, keepdims=True)
    i1 = jnp.min(jnp.where(el == m1, r8, float(MOE_PER_GROUP)), axis=0, keepdims=True)
    el2 = jnp.where(r8 == i1, -jnp.inf, el)
    m2 = jnp.max(el2, axis=0, keepdims=True)
    i2 = jnp.min(jnp.where(el2 == m2, r8, float(MOE_PER_GROUP)), axis=0, keepdims=True)
    e21 = jnp.exp(m2 - m1)
    wa = gval / (1.0 + e21)
    wb = gval * e21 / (1.0 + e21)
    fa = gidx * MOE_PER_GROUP + i1
    fb = gidx * MOE_PER_GROUP + i2
    re = lax.broadcasted_iota(jnp.int32, (MOE_EXPERTS, tm), 0).astype(F32)
    hit_a = re == fa
    hit_b = re == fb
    member = jnp.where(hit_a, 1.0, jnp.where(hit_b, 1.0, 0.0))
    before = _dot(member.astype(BF16), su_ref[...]) + carry_ref[...]
    rank_a = jnp.sum(jnp.where(hit_a, before, 0.0), axis=0, keepdims=True)
    rank_b = jnp.sum(jnp.where(hit_b, before, 0.0), axis=0, keepdims=True)
    carry_ref[...] = carry_ref[...] + jnp.sum(member, axis=1, keepdims=True)
    cnt_ref[...] = jnp.broadcast_to(carry_ref[...], cnt_ref.shape)
    zero = jnp.zeros_like(wa)
    route_ref[...] = jnp.concatenate([fa, fb, wa, wb, rank_a, rank_b, zero, zero], axis=0)


def _out_router(x2d, acts, weights, ffn_gain, wg, bg, we, be):
    t = x2d.shape[0]
    tm = ROUTER_TILE
    n_in = len(acts)
    wr = jnp.concatenate([wg.T, jnp.zeros((8 - MOE_GROUPS, D_MODEL), F32), we.T], axis=0)
    br = jnp.concatenate([bg, jnp.zeros((8 - MOE_GROUPS,), F32), be]).reshape(ROUTE_ROWS, 1).astype(F32)
    ti = jnp.arange(tm)
    su = (ti[:, None] < ti[None, :]).astype(BF16)
    row = lambda i: (i, 0)
    fixed = lambda i: (0, 0)
    in_specs = [pl.BlockSpec((tm, D_MODEL), row)]
    in_specs += [pl.BlockSpec((tm, a.shape[1]), row) for a in acts]
    in_specs += [pl.BlockSpec(w.shape, fixed) for w in weights]
    in_specs += [pl.BlockSpec((1, D_MODEL), fixed), pl.BlockSpec(wr.shape, fixed), pl.BlockSpec(br.shape, fixed),
                 pl.BlockSpec(su.shape, fixed)]
    x1, hn, route, cnt = pl.pallas_call(
        functools.partial(_out_router_kernel, n_in=n_in),
        grid=(t // tm,),
        in_specs=in_specs,
        out_specs=[pl.BlockSpec((tm, D_MODEL), row), pl.BlockSpec((tm, D_MODEL), row),
                   pl.BlockSpec((8, tm), lambda i: (0, i)), pl.BlockSpec((MOE_EXPERTS, LANES), fixed)],
        out_shape=[jax.ShapeDtypeStruct((t, D_MODEL), F32), jax.ShapeDtypeStruct((t, D_MODEL), F32),
                   jax.ShapeDtypeStruct((8, t), F32), jax.ShapeDtypeStruct((MOE_EXPERTS, LANES), F32)],
        scratch_shapes=[pltpu.VMEM((MOE_EXPERTS, 1), F32)],
        compiler_params=_params("arbitrary"),
        name="out_proj_router",
    )(x2d, *acts, *weights, ffn_gain.reshape(1, D_MODEL), wr.astype(F32), br, su)
    return x1, hn, route, cnt[:, 0]


def _row_copy(src_ref, src_row, dst_ref, dst_row, sem):
    return pltpu.make_async_copy(src_ref.at[pl.ds(src_row, 1), :], dst_ref.at[pl.ds(dst_row, 1), :], sem)


ROWS_PER_ISSUE = 8


def _for_rows(n, fn):
    def body(i, c):
        for u in range(ROWS_PER_ISSUE):
            fn(i * ROWS_PER_ISSUE + u)
        return c

    lax.fori_loop(0, n // ROWS_PER_ISSUE, body, 0)


def _dispatch_kernel(sa_ref, sb_ref, hn_ref, xs_in_ref, xs_ref, sem):
    del xs_in_ref
    tm = hn_ref.shape[0]

    def copies(r):
        return (_row_copy(hn_ref, r, xs_ref, sa_ref[0, 0, r], sem), _row_copy(hn_ref, r, xs_ref, sb_ref[0, 0, r], sem))

    def start(r):
        for cp in copies(r):
            cp.start()

    def wait(r):
        for cp in copies(r):
            cp.wait()

    _for_rows(tm, start)
    _for_rows(tm, wait)


def _dispatch(hn, slot_a, slot_b, n_slots):
    t = hn.shape[0]
    tm = ROW_TILE
    smem = lambda: pl.BlockSpec((1, 1, tm), lambda i: (i, 0, 0), memory_space=pltpu.SMEM)
    xs0 = jnp.zeros((n_slots, D_MODEL), hn.dtype)
    return pl.pallas_call(
        _dispatch_kernel,
        grid=(t // tm,),
        in_specs=[smem(), smem(), pl.BlockSpec((tm, D_MODEL), lambda i: (i, 0)), pl.BlockSpec(memory_space=pl.ANY)],
        out_specs=pl.BlockSpec(memory_space=pl.ANY),
        out_shape=jax.ShapeDtypeStruct((n_slots, D_MODEL), hn.dtype),
        scratch_shapes=[pltpu.SemaphoreType.DMA(())],
        input_output_aliases={3: 0},
        compiler_params=_params("arbitrary"),
        name="moe_dispatch",
    )(slot_a.reshape(t // tm, 1, tm), slot_b.reshape(t // tm, 1, tm), hn, xs0)


def _ffn_kernel(te_ref, nv_ref, xs_ref, w1_ref, w3_ref, w2_ref, ys_ref):
    @pl.when(pl.program_id(0) < nv_ref[0])
    def _():
        x = xs_ref[...].astype(BF16)
        a = _dot(x, w1_ref[...])
        b = _dot(x, w3_ref[...])
        ys_ref[...] = _dot((_silu(a) * b).astype(BF16), w2_ref[...])

    @pl.when(pl.program_id(0) >= nv_ref[0])
    def _():
        ys_ref[...] = jnp.zeros_like(ys_ref)


def _expert_ffn(xs, tile_expert, n_valid, w1, w3, w2):
    n_slots = xs.shape[0]
    n_tiles = n_slots // EXPERT_TILE
    wmap = lambda i, te, nv: (te[i], 0, 0)
    grid_spec = pltpu.PrefetchScalarGridSpec(
        num_scalar_prefetch=2,
        grid=(n_tiles,),
        in_specs=[pl.BlockSpec((EXPERT_TILE, D_MODEL), lambda i, te, nv: (i, 0)),
                  pl.BlockSpec((None, D_MODEL, MOE_FF), wmap), pl.BlockSpec((None, D_MODEL, MOE_FF), wmap),
                  pl.BlockSpec((None, MOE_FF, D_MODEL), wmap)],
        out_specs=pl.BlockSpec((EXPERT_TILE, D_MODEL), lambda i, te, nv: (i, 0)),
    )
    return pl.pallas_call(
        _ffn_kernel,
        grid_spec=grid_spec,
        out_shape=jax.ShapeDtypeStruct((n_slots, D_MODEL), F32),
        compiler_params=_params("arbitrary"),
        name="moe_expert_ffn",
    )(tile_expert, n_valid, xs, w1, w3, w2)


def _combine_kernel(sa_ref, sb_ref, x_ref, wt_ref, ys_ref, o_ref, ya_ref, yb_ref, sem):
    tm = x_ref.shape[0]

    def copies(r):
        return (_row_copy(ys_ref, sa_ref[0, 0, r], ya_ref, r, sem), _row_copy(ys_ref, sb_ref[0, 0, r], yb_ref, r, sem))

    def start(r):
        for cp in copies(r):
            cp.start()

    def wait(r):
        for cp in copies(r):
            cp.wait()

    _for_rows(tm, start)
    _for_rows(tm, wait)
    wt = wt_ref[...]
    o_ref[...] = x_ref[...] + wt[:, 2:3] * ya_ref[...] + wt[:, 3:4] * yb_ref[...]


def _combine(x1, ys, slot_a, slot_b, wts):
    t = x1.shape[0]
    tm = ROW_TILE
    smem = lambda: pl.BlockSpec((1, 1, tm), lambda i: (i, 0, 0), memory_space=pltpu.SMEM)
    row = lambda i: (i, 0)
    return pl.pallas_call(
        _combine_kernel,
        grid=(t // tm,),
        in_specs=[smem(), smem(), pl.BlockSpec((tm, D_MODEL), row), pl.BlockSpec((tm, 8), row),
                  pl.BlockSpec(memory_space=pl.ANY)],
        out_specs=pl.BlockSpec((tm, D_MODEL), row),
        out_shape=jax.ShapeDtypeStruct((t, D_MODEL), F32),
        scratch_shapes=[pltpu.VMEM((tm, D_MODEL), F32), pltpu.VMEM((tm, D_MODEL), F32),
                        pltpu.SemaphoreType.DMA(())],
        compiler_params=_params("arbitrary"),
        name="moe_combine",
    )(slot_a.reshape(t // tm, 1, tm), slot_b.reshape(t // tm, 1, tm), x1, wts, ys)


def _moe(x1, hn, route, counts, w1, w3, w2):
    t = x1.shape[0]
    n_tiles = (2 * t) // EXPERT_TILE + MOE_EXPERTS
    n_slots = n_tiles * EXPERT_TILE
    counts = counts.astype(jnp.int32)
    tiles_per = (counts + EXPERT_TILE - 1) // EXPERT_TILE
    tile_end = jnp.cumsum(tiles_per)
    seg_start = (tile_end - tiles_per) * EXPERT_TILE
    n_valid = tile_end[-1:]
    tile_expert = jnp.minimum(jnp.searchsorted(tile_end, jnp.arange(n_tiles, dtype=jnp.int32), side="right"),
                              MOE_EXPERTS - 1).astype(jnp.int32)
    ri = route.astype(jnp.int32)
    slot_a = seg_start[ri[0]] + ri[4]
    slot_b = seg_start[ri[1]] + ri[5]
    xs = _dispatch(hn, slot_a, slot_b, n_slots)
    ys = _expert_ffn(xs, tile_expert, n_valid, w1.astype(BF16), w3.astype(BF16), w2.astype(BF16))
    return _combine(x1, ys, slot_a, slot_b, jnp.transpose(route))


def _odd_proj_kernel(x_ref, g_ref, wz_ref, wx_ref, wd_ref, z_ref, xbc_ref, dt_ref):
    xn = _rms_rows(x_ref[...], g_ref[...]).astype(BF16)
    z_ref[...] = _dot(xn, wz_ref[...])
    xbc_ref[...] = _dot(xn, wx_ref[...])
    dt_ref[...] = _dot(xn, wd_ref[...])


def _odd_proj(x2d, gain, w_in):
    t = x2d.shape[0]
    tm = ROW_TILE
    wz = w_in[:, :SSM_INNER].astype(BF16)
    wx = w_in[:, SSM_INNER:SSM_INNER + SSM_CONV_CH].astype(BF16)
    wd = jnp.pad(w_in[:, SSM_INNER + SSM_CONV_CH:], ((0, 0), (0, LANES - SSM_HEADS))).astype(BF16)
    row = lambda i: (i, 0)
    fixed = lambda i: (0, 0)
    return pl.pallas_call(
        _odd_proj_kernel,
        grid=(t // tm,),
        in_specs=[pl.BlockSpec((tm, D_MODEL), row), pl.BlockSpec((1, D_MODEL), fixed),
                  pl.BlockSpec(wz.shape, fixed), pl.BlockSpec(wx.shape, fixed), pl.BlockSpec(wd.shape, fixed)],
        out_specs=[pl.BlockSpec((tm, SSM_INNER), row), pl.BlockSpec((tm, SSM_CONV_CH), row),
                   pl.BlockSpec((tm, LANES), row)],
        out_shape=[jax.ShapeDtypeStruct((t, SSM_INNER), F32), jax.ShapeDtypeStruct((t, SSM_CONV_CH), F32),
                   jax.ShapeDtypeStruct((t, LANES), F32)],
        compiler_params=_params("parallel"),
        name="odd_proj",
    )(x2d, gain.reshape(1, D_MODEL), wz, wx, wd)


def _ssd_kernel(z_ref, xbc_ref, dt_ref, cw_ref, cb_ref, dtb_ref, a_ref, dskip_ref, ng_ref, tri_ref, exp_ref,
                y_ref, st_ref, ext_ref, xc_ref):
    rows_blk = xbc_ref.shape[0]

    @pl.when(pl.program_id(1) == 0)
    def _():
        st_ref[...] = jnp.zeros_like(st_ref)
        ext_ref[0:SUBLANES, :] = jnp.zeros((SUBLANES, SSM_CONV_CH), F32)

    cur = xbc_ref[...]
    ext_ref[SUBLANES:, :] = cur
    acc = cb_ref[...] + cw_ref[SSM_CONV - 1:SSM_CONV, :] * cur
    for k in range(1, SSM_CONV):
        acc = acc + cw_ref[SSM_CONV - 1 - k:SSM_CONV - k, :] * ext_ref[SUBLANES - k:SUBLANES - k + rows_blk, :]
    ext_ref[0:SUBLANES, :] = cur[rows_blk - SUBLANES:]
    xc_ref[...] = _silu(acc)

    tri = tri_ref[...]
    expand = exp_ref[...]
    a_neg = -jnp.exp(a_ref[...])
    lane = lax.broadcasted_iota(jnp.int32, (CHUNK, LANES), 1)
    trow = lax.broadcasted_iota(jnp.int32, (CHUNK, LANES), 0)
    left = lane < SSM_HEAD_DIM
    diag = trow == jnp.where(left, lane, lane - SSM_HEAD_DIM)
    causal = trow >= jnp.where(left, lane, lane - SSM_HEAD_DIM)
    heads_per_group = SSM_HEADS // SSM_GROUPS
    b_off = SSM_INNER
    c_off = SSM_INNER + SSM_GROUPS * SSM_STATE

    def chunk(c, carry):
        r0 = pl.multiple_of(c * CHUNK, CHUNK)
        rows = pl.ds(r0, CHUNK)
        dt = dt_ref[rows, :] + dtb_ref[...]
        dt = jnp.maximum(dt, 0.0) + jnp.log(1.0 + jnp.exp(-jnp.abs(dt)))
        cum_h = _dot_f32(tri, dt * a_neg)
        c_hi = cum_h.astype(BF16)
        c_r = cum_h - c_hi.astype(F32)
        c_mid = c_r.astype(BF16)
        c_lo = (c_r - c_mid.astype(F32)).astype(BF16)
        cum = _dot(c_hi, expand) + _dot(c_mid, expand) + _dot(c_lo, expand)
        dtx = _dot(dt.astype(BF16), expand)
        xs = xc_ref[rows, 0:SSM_INNER]
        xdt = xs * dtx
        cum_last = cum[CHUNK - 1:CHUNK, :]
        decay_in = jnp.exp(cum)
        decay_out = jnp.exp(cum_last - cum)
        xw = (xdt * decay_out).astype(BF16)
        y_parts = []
        for g in range(SSM_GROUPS):
            gl = slice(g * SSM_GROUP_WIDTH, (g + 1) * SSM_GROUP_WIDTH)
            bm = xc_ref[rows, b_off + g * SSM_STATE:b_off + (g + 1) * SSM_STATE].astype(BF16)
            cm = xc_ref[rows, c_off + g * SSM_STATE:c_off + (g + 1) * SSM_STATE].astype(BF16)
            st = st_ref[g]
            y_g = _dot(cm, st.astype(BF16)) * decay_in[:, gl]
            st_ref[g] = st * jnp.exp(cum_last[:, gl]) + _dot_tn(bm, xw[:, gl])
            cb2 = _dot_nt(cm, jnp.concatenate([bm, bm], axis=0))
            pair_out = []
            for pr in range(heads_per_group // 2):
                sl = slice(g * SSM_GROUP_WIDTH + pr * LANES, g * SSM_GROUP_WIDTH + (pr + 1) * LANES)
                cum_p = cum[:, sl]
                cum_row = jnp.sum(jnp.where(diag, cum_p, 0.0), axis=0, keepdims=True)
                lmat = jnp.where(causal, jnp.exp(jnp.minimum(cum_p - cum_row, 0.0)), 0.0)
                xp = xdt[:, sl]
                x2 = jnp.concatenate([jnp.where(left, xp, 0.0), jnp.where(left, 0.0, xp)], axis=0).astype(BF16)
                pair_out.append(_dot((cb2 * lmat).astype(BF16), x2))
            y_parts.append(y_g + jnp.concatenate(pair_out, axis=1))
        y = jnp.concatenate(y_parts, axis=1) + dskip_ref[...] * xs
        y = y * _silu(z_ref[rows, :])
        outs = []
        for g in range(SSM_GROUPS):
            gl = slice(g * SSM_GROUP_WIDTH, (g + 1) * SSM_GROUP_WIDTH)
            outs.append(_rms_rows(y[:, gl], ng_ref[:, gl]))
        y_ref[rows, :] = jnp.concatenate(outs, axis=1).astype(y_ref.dtype)
        return carry

    lax.fori_loop(0, rows_blk // CHUNK, chunk, 0)


def _ssd(z, xbc, dt, conv_w, conv_b, dt_bias, a_log, d_skip, norm_g, batch, seq):
    t = z.shape[0]
    rb = SSD_BLOCK
    nblk = seq // rb
    tri = jnp.tril(jnp.ones((CHUNK, CHUNK), F32))
    lane = jnp.arange(SSM_INNER)
    expand = (jnp.arange(LANES)[:, None] == (lane[None, :] // SSM_HEAD_DIM)).astype(BF16)
    pad_h = lambda v: jnp.pad(v.astype(F32), (0, LANES - SSM_HEADS)).reshape(1, LANES)
    row = lambda b, s: (b * nblk + s, 0)
    fixed = lambda b, s: (0, 0)
    return pl.pallas_call(
        _ssd_kernel,
        grid=(batch, nblk),
        in_specs=[pl.BlockSpec((rb, SSM_INNER), row), pl.BlockSpec((rb, SSM_CONV_CH), row),
                  pl.BlockSpec((rb, LANES), row),
                  pl.BlockSpec((SSM_CONV, SSM_CONV_CH), fixed), pl.BlockSpec((1, SSM_CONV_CH), fixed),
                  pl.BlockSpec((1, LANES), fixed), pl.BlockSpec((1, LANES), fixed),
                  pl.BlockSpec((1, SSM_INNER), fixed), pl.BlockSpec((1, SSM_INNER), fixed),
                  pl.BlockSpec((CHUNK, CHUNK), fixed), pl.BlockSpec((LANES, SSM_INNER), fixed)],
        out_specs=pl.BlockSpec((rb, SSM_INNER), row),
        out_shape=jax.ShapeDtypeStruct((t, SSM_INNER), BF16),
        scratch_shapes=[pltpu.VMEM((SSM_GROUPS, SSM_STATE, SSM_GROUP_WIDTH), F32),
                        pltpu.VMEM((rb + SUBLANES, SSM_CONV_CH), F32), pltpu.VMEM((rb, SSM_CONV_CH), F32)],
        compiler_params=_params("parallel", "arbitrary"),
        name="ssd_scan",
    )(z, xbc, dt, conv_w.astype(F32), conv_b.reshape(1, SSM_CONV_CH).astype(F32), pad_h(dt_bias), pad_h(a_log),
      jnp.repeat(d_skip.astype(F32), SSM_HEAD_DIM).reshape(1, SSM_INNER), norm_g.reshape(1, SSM_INNER).astype(F32),
      tri, expand)


def kernel(x, mix_norm, ffn_norm, ev_w_in, hg_lb_logits, hg_out_norm, at_q_norm, at_k_norm, ev_w_out, od_w_in,
           od_conv_w, od_conv_b, od_dt_bias, od_A_log, od_D, od_out_norm, od_w_out, moe_wg, moe_bg, moe_we,
           moe_be, moe_w1, moe_w3, moe_w2):
    batch, seq, d = x.shape
    assert d == D_MODEL and seq % HG_BLOCK == 0 and seq % KEY_TILE == 0 and (batch * seq) % ROUTER_TILE == 0
    depth = mix_norm.shape[0]
    x2d = x.reshape(batch * seq, d).astype(F32)
    lbs = jnp.cumsum(jax.nn.softmax(hg_lb_logits.astype(F32), axis=0), axis=0)
    for layer in range(depth):
        i = layer // 2
        if layer % 2 == 0:
            hproj, q, k, v, iq, ik, iw = _even_proj(x2d, mix_norm[layer], ev_w_in[i], at_q_norm[i], at_k_norm[i],
                                                    seq)
            o_h = _hgrn(hproj, lbs[i], hg_out_norm[i], batch, seq)
            o_a = _dsa(q, iq, iw, k, v, ik, batch, seq)
            w_h = ev_w_out[i][:HG_WIDTH].astype(BF16)
            w_a = jnp.pad(ev_w_out[i][HG_WIDTH:].reshape(AT_HEADS, AT_DIM, d),
                          ((0, 0), (0, LANES - AT_DIM), (0, 0))).reshape(AT_HEADS * LANES, d).astype(BF16)
            acts, weights = [o_h, o_a], [w_h, w_a]
        else:
            z, xbc, dt = _odd_proj(x2d, mix_norm[layer], od_w_in[i])
            y = _ssd(z, xbc, dt, od_conv_w[i], od_conv_b[i], od_dt_bias[i], od_A_log[i], od_D[i], od_out_norm[i],
                     batch, seq)
            acts, weights = [y], [od_w_out[i].astype(BF16)]
        x1, hn, route, counts = _out_router(x2d, acts, weights, ffn_norm[layer], moe_wg[layer], moe_bg[layer],
                                            moe_we[layer], moe_be[layer])
        x2d = _moe(x1, hn, route, counts, moe_w1[layer], moe_w3[layer], moe_w2[layer])
    return x2d.reshape(batch, seq, d).astype(x.dtype)
```

```python
import functools
import math

import jax
import jax.numpy as jnp
from jax import lax
from jax.experimental import pallas as pl
from jax.experimental.pallas import tpu as pltpu

D_MODEL = 1024
CHUNK = 64
Q_BLOCK = 128
ROPE_THETA = 10000.0
EPS = 1e-6

HG_HEADS = 4
HG_DIM = 128
HG_WIDTH = HG_HEADS * HG_DIM
AT_HEADS = 8
AT_KV_HEADS = 2
AT_GROUP = AT_HEADS // AT_KV_HEADS
AT_DIM = 64
AT_WIDTH = AT_HEADS * AT_DIM
KV_WIDTH = AT_KV_HEADS * AT_DIM
IDX_HEADS = 4
IDX_DIM = 64
TOPK_MAX = 256

SSM_INNER = 2 * D_MODEL
SSM_HEAD_DIM = 64
SSM_HEADS = SSM_INNER // SSM_HEAD_DIM
SSM_GROUPS = 4
SSM_STATE = 128
SSM_CONV = 4
SSM_CONV_CH = SSM_INNER + 2 * SSM_GROUPS * SSM_STATE
SSM_GROUP_WIDTH = SSM_INNER // SSM_GROUPS

MOE_GROUPS = 4
MOE_PER_GROUP = 8
MOE_EXPERTS = MOE_GROUPS * MOE_PER_GROUP
MOE_FF = 512

LANES = 128
SUBLANES = 8
VMEM_LIMIT = 56 * 1024 * 1024
ROW_TILE = 256
ROUTER_TILE = 512
KEY_TILE = 512
HG_BLOCK = 512
SSD_BLOCK = 256
EXPERT_TILE = 512
ROUTE_ROWS = 8 + MOE_EXPERTS
INT_MIN = -(2 ** 31)
NEG_BIG = -1e30

F32 = jnp.float32
BF16 = jnp.bfloat16
HIGHEST = lax.Precision.HIGHEST


def _params(*sem):
    return pltpu.CompilerParams(dimension_semantics=sem, vmem_limit_bytes=VMEM_LIMIT)


def _dot(a, b):
    return jnp.dot(a, b, preferred_element_type=F32)


def _dot_f32(a, b):
    return jnp.dot(a, b, preferred_element_type=F32, precision=HIGHEST)


def _dot_nt(a, b, precision=None):
    return lax.dot_general(a, b, (((1,), (1,)), ((), ())), preferred_element_type=F32, precision=precision)


def _dot_tn(a, b, precision=None):
    return lax.dot_general(a, b, (((0,), (0,)), ((), ())), preferred_element_type=F32, precision=precision)


def _sigmoid(x):
    return 1.0 / (1.0 + jnp.exp(-x))


def _silu(x):
    return x * _sigmoid(x)


def _rms_rows(x, gain):
    return x * lax.rsqrt(jnp.mean(x * x, axis=-1, keepdims=True) + EPS) * gain


def _rope_slab(x, cos, sin_lo, sin_hi):
    return x * cos + pltpu.roll(x, LANES - 32, 1) * sin_lo + pltpu.roll(x, 32, 1) * sin_hi


def _even_proj_kernel(x_ref, g_ref, wh_ref, wq_ref, wkv_ref, wiq_ref, wik_ref,
                      r1_ref, gm_ref, qn_ref, kn_ref,
                      h_ref, q_ref, k_ref, v_ref, iq_ref, ik_ref, iw_ref):
    xn = _rms_rows(x_ref[...], g_ref[...]).astype(BF16)
    h_ref[...] = _dot(xn, wh_ref[...])

    cos1, lo1, hi1 = r1_ref[0], r1_ref[1], r1_ref[2]
    gm = gm_ref[...]

    def qk_norm(v, gain):
        ms = _dot_f32(v * v, gm)
        return v * lax.rsqrt(ms + EPS) * gain

    q = _dot(xn, wq_ref[...])
    for h in range(AT_HEADS):
        sl = slice(h * LANES, (h + 1) * LANES)
        qh = _rope_slab(qk_norm(q[:, sl], qn_ref[...]), cos1, lo1, hi1)
        q_ref[:, sl] = (qh * (AT_DIM ** -0.5)).astype(BF16)
    kv = _dot(xn, wkv_ref[...])
    for h in range(AT_KV_HEADS):
        sl = slice(h * LANES, (h + 1) * LANES)
        k_ref[:, sl] = _rope_slab(qk_norm(kv[:, sl], kn_ref[...]), cos1, lo1, hi1).astype(BF16)
    v = kv[:, AT_KV_HEADS * LANES:]
    lane = lax.broadcasted_iota(jnp.int32, v.shape, 1)
    v_ref[...] = jnp.where(lane % LANES == AT_DIM, 1.0, v).astype(BF16)
    iq = _dot(xn, wiq_ref[...])
    for h in range(IDX_HEADS):
        sl = slice(h * LANES, (h + 1) * LANES)
        iq_ref[:, sl] = (_rope_slab(iq[:, sl], cos1, lo1, hi1) * (IDX_DIM ** -0.5)).astype(BF16)
    ikw = _rope_slab(_dot(xn, wik_ref[...]), cos1, lo1, hi1)
    ik_ref[...] = ikw.astype(BF16)
    iw_ref[...] = ikw


def _pad_heads(w, heads):
    d = w.shape[0]
    w = w.reshape(d, heads, 64)
    return jnp.pad(w, ((0, 0), (0, 0), (0, 64))).reshape(d, heads * LANES)


def _rope_tables(seq):
    half = AT_DIM // 2
    inv = jnp.exp(-math.log(ROPE_THETA) * jnp.arange(half, dtype=F32) / half)
    ang = jnp.arange(seq, dtype=F32)[:, None] * inv[None, :]
    c, s = jnp.cos(ang), jnp.sin(ang)
    z, o = jnp.zeros_like(c), jnp.ones_like(c)
    return jnp.stack([jnp.concatenate([c, c, o, o], 1), jnp.concatenate([-s, z, z, z], 1),
                      jnp.concatenate([z, s, z, z], 1)])


def _even_proj(x2d, gain, w_in, q_norm, k_norm, seq):
    t = x2d.shape[0]
    tm = ROW_TILE
    o = [0, 4 * HG_WIDTH]
    for wdt in (AT_WIDTH, KV_WIDTH, KV_WIDTH, IDX_HEADS * IDX_DIM, IDX_DIM, IDX_HEADS):
        o.append(o[-1] + wdt)
    wh = w_in[:, o[0]:o[1]].astype(BF16)
    wq = _pad_heads(w_in[:, o[1]:o[2]], AT_HEADS).astype(BF16)
    wkv = jnp.concatenate([_pad_heads(w_in[:, o[2]:o[3]], AT_KV_HEADS),
                           _pad_heads(w_in[:, o[3]:o[4]], AT_KV_HEADS)], 1).astype(BF16)
    wiq = _pad_heads(w_in[:, o[4]:o[5]], IDX_HEADS).astype(BF16)
    wik = jnp.pad(w_in[:, o[5]:o[7]], ((0, 0), (0, LANES - IDX_DIM - IDX_HEADS))).astype(BF16)
    rope1 = _rope_tables(seq)
    lane = jnp.arange(LANES)
    gm = jnp.where((lane[:, None] < AT_DIM) & (lane[None, :] < AT_DIM), 1.0 / AT_DIM, 0.0).astype(F32)
    pad_gain = lambda g: jnp.pad(g, (0, LANES - AT_DIM), constant_values=1.0).reshape(1, LANES)
    nseq = seq // tm
    row = lambda i: (i, 0)
    fixed = lambda i: (0, 0)
    rope_map = lambda i: (0, i % nseq, 0)
    outs = pl.pallas_call(
        _even_proj_kernel,
        grid=(t // tm,),
        in_specs=[pl.BlockSpec((tm, D_MODEL), row), pl.BlockSpec((1, D_MODEL), fixed),
                  pl.BlockSpec(wh.shape, fixed), pl.BlockSpec(wq.shape, fixed), pl.BlockSpec(wkv.shape, fixed),
                  pl.BlockSpec(wiq.shape, fixed), pl.BlockSpec(wik.shape, fixed),
                  pl.BlockSpec((3, tm, LANES), rope_map),
                  pl.BlockSpec((LANES, LANES), fixed), pl.BlockSpec((1, LANES), fixed),
                  pl.BlockSpec((1, LANES), fixed)],
        out_specs=[pl.BlockSpec((tm, 4 * HG_WIDTH), row), pl.BlockSpec((tm, AT_HEADS * LANES), row),
                   pl.BlockSpec((tm, AT_KV_HEADS * LANES), row), pl.BlockSpec((tm, AT_KV_HEADS * LANES), row),
                   pl.BlockSpec((tm, IDX_HEADS * LANES), row),
                   pl.BlockSpec((tm, LANES), row), pl.BlockSpec((tm, LANES), row)],
        out_shape=[jax.ShapeDtypeStruct((t, 4 * HG_WIDTH), F32), jax.ShapeDtypeStruct((t, AT_HEADS * LANES), BF16),
                   jax.ShapeDtypeStruct((t, AT_KV_HEADS * LANES), BF16),
                   jax.ShapeDtypeStruct((t, AT_KV_HEADS * LANES), BF16),
                   jax.ShapeDtypeStruct((t, IDX_HEADS * LANES), BF16),
                   jax.ShapeDtypeStruct((t, LANES), BF16), jax.ShapeDtypeStruct((t, LANES), F32)],
        compiler_params=_params("parallel"),
        name="even_proj",
    )(x2d, gain.reshape(1, D_MODEL), wh, wq, wkv, wiq, wik, rope1, gm, pad_gain(q_norm), pad_gain(k_norm))
    return outs


HG_SUB = 16
HG_SAFE_SPAN = 60.0


def _hgrn_kernel(h_ref, lb_ref, gn_ref, tri_ref, o_ref, st_ref, od_ref):
    @pl.when(pl.program_id(1) == 0)
    def _():
        st_ref[...] = jnp.zeros_like(st_ref)

    tri = tri_ref[...]
    s_idx = lax.broadcasted_iota(jnp.int32, (HG_SUB, 1), 0)

    t_idx = lax.broadcasted_iota(jnp.int32, (CHUNK, CHUNK), 0)
    c_idx = lax.broadcasted_iota(jnp.int32, (CHUNK, CHUNK), 1)

    def col(stream, h):
        return slice((stream * HG_HEADS + h) * HG_DIM, (stream * HG_HEADS + h + 1) * HG_DIM)

    def chunk(c, carry):
        r0 = pl.multiple_of(c * CHUNK, CHUNK)
        rows = pl.ds(r0, CHUNK)
        gates = []
        for h in range(HG_HEADS):
            lb = lb_ref[:, h * HG_DIM:(h + 1) * HG_DIM]
            hf = h_ref[rows, col(1, h)]
            f = lb + (1.0 - lb) * _sigmoid(hf)
            kk = (1.0 - lb) * _sigmoid(-hf)
            gates.append((kk, _dot_f32(tri, jnp.log(f))))
        floor = jnp.min(jnp.concatenate([b[CHUNK - 1:CHUNK] for _, b in gates], axis=1))

        @pl.when(floor >= -HG_SAFE_SPAN)
        def _():
            for h in range(HG_HEADS):
                head_chunk(rows, h, *gates[h], intra_split)

        @pl.when(floor < -HG_SAFE_SPAN)
        def _():
            for h in range(HG_HEADS):
                head_chunk(rows, h, *gates[h], intra_exact)

        return carry

    def intra_split(h, q, kk, v, b):
        del h
        a = _dot_nt((q * jnp.exp(b)).astype(BF16), (kk * jnp.exp(-b)).astype(BF16))
        return _dot(jnp.where(c_idx <= t_idx, a, 0.0).astype(BF16), v.astype(BF16))

    def head_chunk(rows, h, kk, b, intra):
        sl = slice(h * HG_DIM, (h + 1) * HG_DIM)
        q = h_ref[rows, col(0, h)]
        v = h_ref[rows, col(2, h)]
        st = st_ref[h]
        o = _dot_nt((q * jnp.exp(b)).astype(BF16), st.astype(BF16)) + intra(h, q, kk, v, b)
        b_last = b[CHUNK - 1:CHUNK, :]
        st_ref[h] = st * jnp.exp(b_last) + _dot_tn(v.astype(BF16), (kk * jnp.exp(b_last - b)).astype(BF16))
        o = _rms_rows(o, gn_ref[:, sl]) * _silu(h_ref[rows, col(3, h)])
        o_ref[rows, sl] = o.astype(o_ref.dtype)

    def intra_exact(h, q, kk, v, b):
        od = od_ref.at[h]
        off = []
        for blk in range(CHUNK // HG_SUB):
            lo, hi = blk * HG_SUB, (blk + 1) * HG_SUB
            bs, ks, vs = b[lo:hi], kk[lo:hi], v[lo:hi]
            for tt in range(HG_SUB):
                t = lo + tt
                e = jnp.exp(jnp.minimum(b[t:t + 1] - bs, 0.0))
                a = jnp.sum(e * (q[t:t + 1] * ks), axis=-1, keepdims=True)
                a = jnp.where(s_idx <= tt, a, 0.0)
                od[t:t + 1, :] = jnp.sum(a * vs, axis=0, keepdims=True)
            if blk == 0:
                off.append(jnp.zeros((HG_SUB, HG_DIM), F32))
            else:
                ref = b[lo - 1:lo]
                qd = (q[lo:hi] * jnp.exp(b[lo:hi] - ref)).astype(BF16)
                kd = (kk[:lo] * jnp.exp(ref - b[:lo])).astype(BF16)
                off.append(_dot(_dot_nt(qd, kd).astype(BF16), v[:lo].astype(BF16)))
        return od[...] + jnp.concatenate(off, axis=0)

    lax.fori_loop(0, HG_BLOCK // CHUNK, chunk, 0)


def _hgrn(hproj, lb, out_norm, batch, seq):
    t = hproj.shape[0]
    nblk = seq // HG_BLOCK
    tri = jnp.tril(jnp.ones((CHUNK, CHUNK), F32))
    row = lambda b, s: (b * nblk + s, 0)
    fixed = lambda b, s: (0, 0)
    return pl.pallas_call(
        _hgrn_kernel,
        grid=(batch, nblk),
        in_specs=[pl.BlockSpec((HG_BLOCK, 4 * HG_WIDTH), row), pl.BlockSpec((1, HG_WIDTH), fixed),
                  pl.BlockSpec((1, HG_WIDTH), fixed), pl.BlockSpec((CHUNK, CHUNK), fixed)],
        out_specs=pl.BlockSpec((HG_BLOCK, HG_WIDTH), row),
        out_shape=jax.ShapeDtypeStruct((t, HG_WIDTH), BF16),
        scratch_shapes=[pltpu.VMEM((HG_HEADS, HG_DIM, HG_DIM), F32), pltpu.VMEM((HG_HEADS, CHUNK, HG_DIM), F32)],
        compiler_params=_params("parallel", "arbitrary"),
        name="hgrn2_scan",
    )(hproj, lb.reshape(1, HG_WIDTH), out_norm.reshape(1, HG_WIDTH), tri)


COUNT_LANES = 4
SEARCH_UNTESTED_BITS = 18


def _dsa_kernel(q_ref, iq_ref, iw_ref, k_ref, vt_ref, ik_ref, tril_ref, o_ref, keys_ref, *, topk):
    j = pl.program_id(1)
    kb = KEY_TILE
    nkt = (j * Q_BLOCK + Q_BLOCK + kb - 1) // kb
    qrow = lax.broadcasted_iota(jnp.int32, (1, Q_BLOCK), 1)
    nadm = ((j * Q_BLOCK + qrow) // CHUNK + 1) * CHUNK
    key_row = lax.broadcasted_iota(jnp.int32, (kb, Q_BLOCK), 0)
    iwt = jnp.transpose(iw_ref[...])
    iws = [iwt[IDX_DIM + h:IDX_DIM + h + 1, :] for h in range(IDX_HEADS)]

    iq_pairs = [jnp.concatenate([iq_ref[:, (2 * p + u) * LANES:(2 * p + u + 1) * LANES] for u in range(2)], axis=0)
                for p in range(IDX_HEADS // 2)]

    def score_tile(kt, carry):
        ik = ik_ref[kt]
        score = jnp.zeros((kb, Q_BLOCK), F32)
        for p in range(IDX_HEADS // 2):
            sc = _dot_nt(ik, iq_pairs[p])
            for u in range(2):
                score = score + iws[2 * p + u] * jnp.maximum(sc[:, u * Q_BLOCK:(u + 1) * Q_BLOCK], 0.0)
        bits = pltpu.bitcast(score, jnp.int32)
        key = jnp.where(bits < 0, bits ^ 0x7FFFFFFF, bits)
        key = jnp.where(bits == INT_MIN, 0, key)
        keys_ref[kt] = jnp.where(kt * kb + key_row < nadm, key, INT_MIN)
        return carry

    lax.fori_loop(0, nkt, score_tile, 0)

    @pl.when(nkt % 2 == 1)
    def _():
        keys_ref[nkt] = jnp.full((kb, Q_BLOCK), INT_MIN, jnp.int32)

    def fold(ind):
        return jnp.sum(ind.reshape(COUNT_LANES, kb // (COUNT_LANES * SUBLANES), SUBLANES, Q_BLOCK), axis=1)

    def count(indicator):
        def body(p, acc):
            return acc + fold(indicator(keys_ref[2 * p])) + fold(indicator(keys_ref[2 * p + 1]))
        acc = lax.fori_loop(0, (nkt + 1) // 2, body, jnp.zeros((COUNT_LANES, SUBLANES, Q_BLOCK), F32))
        return jnp.sum(jnp.sum(acc, axis=0), axis=0, keepdims=True)

    zero_ge = count(lambda ks: jnp.where(ks >= 0, 1.0, 0.0))
    zero_gt = count(lambda ks: jnp.where(ks > 0, 1.0, 0.0))
    at_zero = jnp.where(zero_gt < topk, jnp.where(zero_ge >= topk, 1.0, 0.0), 0.0)
    settled0 = jnp.where(nadm < topk, 1.0, at_zero)
    found0 = jnp.where(nadm < topk, INT_MIN, 0)

    def search_bit(i, st):
        tu, found, settled = st
        cand = tu | jnp.left_shift(jnp.int32(1), 31 - i)
        cand_s = cand ^ INT_MIN
        cnt = count(lambda ks: jnp.where(ks >= cand_s, 1.0, 0.0))
        hit = jnp.where(settled > 0.5, 0.0, jnp.where(cnt == topk, 1.0, 0.0))
        found = jnp.where(hit > 0.5, cand_s, found)
        settled = jnp.maximum(settled, hit)
        tu = jnp.where(cnt >= topk, cand, tu)
        return tu, found, settled

    tu0 = jnp.where(zero_ge >= topk, INT_MIN, 0)
    st = lax.fori_loop(1, SEARCH_UNTESTED_BITS, search_bit, (tu0, found0, settled0))

    def search_cond(c):
        i, _, open_rows = c
        return jnp.logical_and(i < 32, open_rows > 0.5)

    def search_step(c):
        i, st, _ = c
        st = search_bit(i, st)
        return i + 1, st, jnp.max(1.0 - st[2])

    _, (tu, found, settled), _ = lax.while_loop(
        search_cond, search_step, (jnp.int32(SEARCH_UNTESTED_BITS), st, jnp.max(1.0 - st[2])))
    thr = jnp.where(settled > 0.5, found, tu ^ INT_MIN)
    need = topk - count(lambda ks: jnp.where(ks > thr, 1.0, 0.0))

    rows4 = AT_GROUP * Q_BLOCK
    qs = [jnp.concatenate([q_ref[:, (g * AT_GROUP + hh) * LANES:(g * AT_GROUP + hh + 1) * LANES]
                           for hh in range(AT_GROUP)], axis=0) for g in range(AT_KV_HEADS)]

    def masked_scores(kt, seen):
        ks = keys_ref[kt]
        tied = jnp.where(ks == thr, 1.0, 0.0)
        rank = _dot(tril_ref[...], tied.astype(BF16)) + seen
        tie = jnp.where(ks == thr, jnp.where(rank <= need, 0.0, NEG_BIG), NEG_BIG)
        bias = jnp.where(kt * kb + key_row < nadm, jnp.where(ks > thr, 0.0, tie), NEG_BIG)
        bias4 = jnp.concatenate([bias] * AT_GROUP, axis=1)
        k_tile = k_ref[kt]
        return rank[kb - 1:kb, :], tuple(_dot_nt(k_tile[:, g * LANES:(g + 1) * LANES], qs[g]) + bias4
                                         for g in range(AT_KV_HEADS))

    def att_tile(kt, carry):
        seen, state = carry
        seen, scores = masked_scores(kt, seen)
        vt_tile = vt_ref[kt]
        out = []
        for g in range(AT_KV_HEADS):
            m, acc = state[g]
            s = scores[g]
            m_new = jnp.maximum(m, jnp.max(s, axis=0, keepdims=True))
            alpha = jnp.exp(m - m_new)
            p = jnp.exp(s - m_new).astype(BF16)
            acc = alpha * acc + _dot(vt_tile[g * LANES:(g + 1) * LANES, :], p)
            out.append((m_new, acc))
        return seen, tuple(out)

    init = tuple((jnp.full((1, rows4), 0.1 * NEG_BIG, F32), jnp.zeros((LANES, rows4), F32))
                 for _ in range(AT_KV_HEADS))
    _, res = lax.fori_loop(0, (nkt + 1) // 2, lambda p, c: att_tile(2 * p + 1, att_tile(2 * p, c)),
                           (jnp.zeros((1, Q_BLOCK), F32), init))
    for g in range(AT_KV_HEADS):
        _, acc = res[g]
        og = acc / acc[AT_DIM:AT_DIM + 1, :]
        for hh in range(AT_GROUP):
            h = g * AT_GROUP + hh
            o_ref[:, h * LANES:(h + 1) * LANES] = jnp.transpose(og[:, hh * Q_BLOCK:(hh + 1) * Q_BLOCK]).astype(o_ref.dtype)


def _dsa(q, iq, iw, k, v, ik, batch, seq):
    t = q.shape[0]
    nb = seq // Q_BLOCK
    nkt = seq // KEY_TILE
    topk = min(TOPK_MAX, seq // 4)
    width = AT_KV_HEADS * LANES
    k3 = k.reshape(batch * nkt, KEY_TILE, width)
    vt3 = jnp.swapaxes(v.reshape(batch * nkt, KEY_TILE, width), 1, 2)
    ik3 = ik.reshape(batch * nkt, KEY_TILE, LANES)
    ki = jnp.arange(KEY_TILE)
    tril = (ki[None, :] <= ki[:, None]).astype(BF16)
    qrow = lambda b, j: (b * nb + j, 0)
    seq_blk = lambda b, j: (b, 0, 0)
    return pl.pallas_call(
        functools.partial(_dsa_kernel, topk=topk),
        grid=(batch, nb),
        in_specs=[pl.BlockSpec((Q_BLOCK, AT_HEADS * LANES), qrow), pl.BlockSpec((Q_BLOCK, IDX_HEADS * LANES), qrow),
                  pl.BlockSpec((Q_BLOCK, LANES), qrow),
                  pl.BlockSpec((nkt, KEY_TILE, width), seq_blk), pl.BlockSpec((nkt, width, KEY_TILE), seq_blk),
                  pl.BlockSpec((nkt, KEY_TILE, LANES), seq_blk),
                  pl.BlockSpec((KEY_TILE, KEY_TILE), lambda b, j: (0, 0))],
        out_specs=pl.BlockSpec((Q_BLOCK, AT_HEADS * LANES), qrow),
        out_shape=jax.ShapeDtypeStruct((t, AT_HEADS * LANES), BF16),
        scratch_shapes=[pltpu.VMEM((nkt + 1, KEY_TILE, Q_BLOCK), jnp.int32)],
        compiler_params=_params("parallel", "arbitrary"),
        name="dsa_attention",
    )(q, iq, iw, k3, vt3, ik3, tril)


def _out_router_kernel(*refs, n_in):
    x_ref = refs[0]
    a_refs = refs[1:1 + n_in]
    w_refs = refs[1 + n_in:1 + 2 * n_in]
    g_ref, wr_ref, br_ref, su_ref = refs[1 + 2 * n_in:5 + 2 * n_in]
    x1_ref, hn_ref, route_ref, cnt_ref = refs[5 + 2 * n_in:9 + 2 * n_in]
    carry_ref = refs[9 + 2 * n_in]

    @pl.when(pl.program_id(0) == 0)
    def _():
        carry_ref[...] = jnp.zeros_like(carry_ref)

    x1 = x_ref[...]
    for a_ref, w_ref in zip(a_refs, w_refs):
        x1 = x1 + _dot(a_ref[...], w_ref[...])
    x1_ref[...] = x1
    hn = _rms_rows(x1, g_ref[...])
    hn_ref[...] = hn

    lt = _dot_nt(wr_ref[...], hn, precision=HIGHEST) + br_ref[...]
    tm = lt.shape[1]
    r4 = lax.broadcasted_iota(jnp.int32, (MOE_GROUPS, tm), 0).astype(F32)
    gl = lt[0:MOE_GROUPS]
    gmax = jnp.max(gl, axis=0, keepdims=True)
    gidx = jnp.min(jnp.where(gl == gmax, r4, float(MOE_GROUPS)), axis=0, keepdims=True)
    gval = 1.0 / jnp.sum(jnp.exp(gl - gmax), axis=0, keepdims=True)
    el = jnp.zeros((MOE_PER_GROUP, tm), F32)
    for g in range(MOE_GROUPS):
        el = el + jnp.where(gidx == float(g), lt[8 + g * MOE_PER_GROUP:8 + (g + 1) * MOE_PER_GROUP], 0.0)
    r8 = lax.broadcasted_iota(jnp.int32, (MOE_PER_GROUP, tm), 0).astype(F32)
    m1 = jnp.max(el, axis=0, keepdims=True)
    i1 = jnp.min(jnp.where(el == m1, r8, float(MOE_PER_GROUP)), axis=0, keepdims=True)
    el2 = jnp.where(r8 == i1, -jnp.inf, el)
    m2 = jnp.max(el2, axis=0, keepdims=True)
    i2 = jnp.min(jnp.where(el2 == m2, r8, float(MOE_PER_GROUP)), axis=0, keepdims=True)
    e21 = jnp.exp(m2 - m1)
    wa = gval / (1.0 + e21)
    wb = gval * e21 / (1.0 + e21)
    fa = gidx * MOE_PER_GROUP + i1
    fb = gidx * MOE_PER_GROUP + i2
    re = lax.broadcasted_iota(jnp.int32, (MOE_EXPERTS, tm), 0).astype(F32)
    hit_a = re == fa
    hit_b = re == fb
    member = jnp.where(hit_a, 1.0, jnp.where(hit_b, 1.0, 0.0))
    before = _dot(member.astype(BF16), su_ref[...]) + carry_ref[...]
    rank_a = jnp.sum(jnp.where(hit_a, before, 0.0), axis=0, keepdims=True)
    rank_b = jnp.sum(jnp.where(hit_b, before, 0.0), axis=0, keepdims=True)
    carry_ref[...] = carry_ref[...] + jnp.sum(member, axis=1, keepdims=True)
    cnt_ref[...] = jnp.broadcast_to(carry_ref[...], cnt_ref.shape)
    zero = jnp.zeros_like(wa)
    route_ref[...] = jnp.concatenate([fa, fb, wa, wb, rank_a, rank_b, zero, zero], axis=0)


def _out_router(x2d, acts, weights, ffn_gain, wg, bg, we, be):
    t = x2d.shape[0]
    tm = ROUTER_TILE
    n_in = len(acts)
    wr = jnp.concatenate([wg.T, jnp.zeros((8 - MOE_GROUPS, D_MODEL), F32), we.T], axis=0)
    br = jnp.concatenate([bg, jnp.zeros((8 - MOE_GROUPS,), F32), be]).reshape(ROUTE_ROWS, 1).astype(F32)
    ti = jnp.arange(tm)
    su = (ti[:, None] < ti[None, :]).astype(BF16)
    row = lambda i: (i, 0)
    fixed = lambda i: (0, 0)
    in_specs = [pl.BlockSpec((tm, D_MODEL), row)]
    in_specs += [pl.BlockSpec((tm, a.shape[1]), row) for a in acts]
    in_specs += [pl.BlockSpec(w.shape, fixed) for w in weights]
    in_specs += [pl.BlockSpec((1, D_MODEL), fixed), pl.BlockSpec(wr.shape, fixed), pl.BlockSpec(br.shape, fixed),
                 pl.BlockSpec(su.shape, fixed)]
    x1, hn, route, cnt = pl.pallas_call(
        functools.partial(_out_router_kernel, n_in=n_in),
        grid=(t // tm,),
        in_specs=in_specs,
        out_specs=[pl.BlockSpec((tm, D_MODEL), row), pl.BlockSpec((tm, D_MODEL), row),
                   pl.BlockSpec((8, tm), lambda i: (0, i)), pl.BlockSpec((MOE_EXPERTS, LANES), fixed)],
        out_shape=[jax.ShapeDtypeStruct((t, D_MODEL), F32), jax.ShapeDtypeStruct((t, D_MODEL), F32),
                   jax.ShapeDtypeStruct((8, t), F32), jax.ShapeDtypeStruct((MOE_EXPERTS, LANES), F32)],
        scratch_shapes=[pltpu.VMEM((MOE_EXPERTS, 1), F32)],
        compiler_params=_params("arbitrary"),
        name="out_proj_router",
    )(x2d, *acts, *weights, ffn_gain.reshape(1, D_MODEL), wr.astype(F32), br, su)
    return x1, hn, route, cnt[:, 0]


def _row_copy(src_ref, src_row, dst_ref, dst_row, sem):
    return pltpu.make_async_copy(src_ref.at[pl.ds(src_row, 1), :], dst_ref.at[pl.ds(dst_row, 1), :], sem)


ROWS_PER_ISSUE = 8


def _for_rows(n, fn):
    def body(i, c):
        for u in range(ROWS_PER_ISSUE):
            fn(i * ROWS_PER_ISSUE + u)
        return c

    lax.fori_loop(0, n // ROWS_PER_ISSUE, body, 0)


def _dispatch_kernel(sa_ref, sb_ref, hn_ref, xs_in_ref, xs_ref, sem):
    del xs_in_ref
    tm = hn_ref.shape[0]

    def copies(r):
        return (_row_copy(hn_ref, r, xs_ref, sa_ref[0, 0, r], sem), _row_copy(hn_ref, r, xs_ref, sb_ref[0, 0, r], sem))

    def start(r):
        for cp in copies(r):
            cp.start()

    def wait(r):
        for cp in copies(r):
            cp.wait()

    _for_rows(tm, start)
    _for_rows(tm, wait)


def _dispatch(hn, slot_a, slot_b, n_slots):
    t = hn.shape[0]
    tm = ROW_TILE
    smem = lambda: pl.BlockSpec((1, 1, tm), lambda i: (i, 0, 0), memory_space=pltpu.SMEM)
    xs0 = jnp.zeros((n_slots, D_MODEL), hn.dtype)
    return pl.pallas_call(
        _dispatch_kernel,
        grid=(t // tm,),
        in_specs=[smem(), smem(), pl.BlockSpec((tm, D_MODEL), lambda i: (i, 0)), pl.BlockSpec(memory_space=pl.ANY)],
        out_specs=pl.BlockSpec(memory_space=pl.ANY),
        out_shape=jax.ShapeDtypeStruct((n_slots, D_MODEL), hn.dtype),
        scratch_shapes=[pltpu.SemaphoreType.DMA(())],
        input_output_aliases={3: 0},
        compiler_params=_params("arbitrary"),
        name="moe_dispatch",
    )(slot_a.reshape(t // tm, 1, tm), slot_b.reshape(t // tm, 1, tm), hn, xs0)


def _ffn_kernel(te_ref, nv_ref, xs_ref, w1_ref, w3_ref, w2_ref, ys_ref, b1_ref, b3_ref, b2_ref):
    i = pl.program_id(0)

    @pl.when(jnp.logical_or(i == 0, te_ref[i] != te_ref[jnp.maximum(i - 1, 0)]))
    def _():
        b1_ref[...] = w1_ref[...].astype(BF16)
        b3_ref[...] = w3_ref[...].astype(BF16)
        b2_ref[...] = w2_ref[...].astype(BF16)

    @pl.when(i < nv_ref[0])
    def _():
        x = xs_ref[...].astype(BF16)
        a = _dot(x, b1_ref[...])
        b = _dot(x, b3_ref[...])
        ys_ref[...] = _dot((_silu(a) * b).astype(BF16), b2_ref[...])

    @pl.when(pl.program_id(0) >= nv_ref[0])
    def _():
        ys_ref[...] = jnp.zeros_like(ys_ref)


def _expert_ffn(xs, tile_expert, n_valid, w1, w3, w2):
    n_slots = xs.shape[0]
    n_tiles = n_slots // EXPERT_TILE
    wmap = lambda i, te, nv: (te[i], 0, 0)
    grid_spec = pltpu.PrefetchScalarGridSpec(
        num_scalar_prefetch=2,
        grid=(n_tiles,),
        in_specs=[pl.BlockSpec((EXPERT_TILE, D_MODEL), lambda i, te, nv: (i, 0)),
                  pl.BlockSpec((None, D_MODEL, MOE_FF), wmap), pl.BlockSpec((None, D_MODEL, MOE_FF), wmap),
                  pl.BlockSpec((None, MOE_FF, D_MODEL), wmap)],
        out_specs=pl.BlockSpec((EXPERT_TILE, D_MODEL), lambda i, te, nv: (i, 0)),
        scratch_shapes=[pltpu.VMEM((D_MODEL, MOE_FF), BF16), pltpu.VMEM((D_MODEL, MOE_FF), BF16),
                        pltpu.VMEM((MOE_FF, D_MODEL), BF16)],
    )
    return pl.pallas_call(
        _ffn_kernel,
        grid_spec=grid_spec,
        out_shape=jax.ShapeDtypeStruct((n_slots, D_MODEL), F32),
        compiler_params=_params("arbitrary"),
        name="moe_expert_ffn",
    )(tile_expert, n_valid, xs, w1, w3, w2)


def _combine_kernel(sa_ref, sb_ref, x_ref, wt_ref, ys_ref, o_ref, ya_ref, yb_ref, sem):
    tm = x_ref.shape[0]

    def copies(r):
        return (_row_copy(ys_ref, sa_ref[0, 0, r], ya_ref, r, sem), _row_copy(ys_ref, sb_ref[0, 0, r], yb_ref, r, sem))

    def start(r):
        for cp in copies(r):
            cp.start()

    def wait(r):
        for cp in copies(r):
            cp.wait()

    _for_rows(tm, start)
    _for_rows(tm, wait)
    wt = wt_ref[...]
    o_ref[...] = x_ref[...] + wt[:, 2:3] * ya_ref[...] + wt[:, 3:4] * yb_ref[...]


def _combine(x1, ys, slot_a, slot_b, wts):
    t = x1.shape[0]
    tm = ROW_TILE
    smem = lambda: pl.BlockSpec((1, 1, tm), lambda i: (i, 0, 0), memory_space=pltpu.SMEM)
    row = lambda i: (i, 0)
    return pl.pallas_call(
        _combine_kernel,
        grid=(t // tm,),
        in_specs=[smem(), smem(), pl.BlockSpec((tm, D_MODEL), row), pl.BlockSpec((tm, 8), row),
                  pl.BlockSpec(memory_space=pl.ANY)],
        out_specs=pl.BlockSpec((tm, D_MODEL), row),
        out_shape=jax.ShapeDtypeStruct((t, D_MODEL), F32),
        scratch_shapes=[pltpu.VMEM((tm, D_MODEL), F32), pltpu.VMEM((tm, D_MODEL), F32),
                        pltpu.SemaphoreType.DMA(())],
        compiler_params=_params("arbitrary"),
        name="moe_combine",
    )(slot_a.reshape(t // tm, 1, tm), slot_b.reshape(t // tm, 1, tm), x1, wts, ys)


def _moe(x1, hn, route, counts, w1, w3, w2):
    t = x1.shape[0]
    n_tiles = (2 * t) // EXPERT_TILE + MOE_EXPERTS
    n_slots = n_tiles * EXPERT_TILE
    counts = counts.astype(jnp.int32)
    tiles_per = (counts + EXPERT_TILE - 1) // EXPERT_TILE
    ei = jnp.arange(MOE_EXPERTS, dtype=jnp.int32)
    tile_end = jnp.sum(jnp.where(ei[None, :] <= ei[:, None], tiles_per[None, :], 0), axis=1)
    seg_start = (tile_end - tiles_per) * EXPERT_TILE
    n_valid = tile_end[-1:]
    tile_idx = jnp.arange(n_tiles, dtype=jnp.int32)
    tile_expert = jnp.minimum(jnp.sum((tile_idx[:, None] >= tile_end[None, :]).astype(jnp.int32), axis=1),
                              MOE_EXPERTS - 1)
    ri = route.astype(jnp.int32)
    start_of = lambda e: jnp.sum(jnp.where(e[:, None] == ei[None, :], seg_start[None, :], 0), axis=1)
    slot_a = start_of(ri[0]) + ri[4]
    slot_b = start_of(ri[1]) + ri[5]
    xs = _dispatch(hn, slot_a, slot_b, n_slots)
    ys = _expert_ffn(xs, tile_expert, n_valid, w1, w3, w2)
    return _combine(x1, ys, slot_a, slot_b, jnp.transpose(route))


def _odd_proj_kernel(x_ref, g_ref, wz_ref, wx_ref, wd_ref, z_ref, xbc_ref, dt_ref):
    xn = _rms_rows(x_ref[...], g_ref[...]).astype(BF16)
    z_ref[...] = _dot(xn, wz_ref[...])
    xbc_ref[...] = _dot(xn, wx_ref[...])
    dt_ref[...] = _dot(xn, wd_ref[...])


def _odd_proj(x2d, gain, w_in):
    t = x2d.shape[0]
    tm = ROW_TILE
    wz = w_in[:, :SSM_INNER].astype(BF16)
    wx = w_in[:, SSM_INNER:SSM_INNER + SSM_CONV_CH].astype(BF16)
    wd = jnp.pad(w_in[:, SSM_INNER + SSM_CONV_CH:], ((0, 0), (0, LANES - SSM_HEADS))).astype(BF16)
    row = lambda i: (i, 0)
    fixed = lambda i: (0, 0)
    return pl.pallas_call(
        _odd_proj_kernel,
        grid=(t // tm,),
        in_specs=[pl.BlockSpec((tm, D_MODEL), row), pl.BlockSpec((1, D_MODEL), fixed),
                  pl.BlockSpec(wz.shape, fixed), pl.BlockSpec(wx.shape, fixed), pl.BlockSpec(wd.shape, fixed)],
        out_specs=[pl.BlockSpec((tm, SSM_INNER), row), pl.BlockSpec((tm, SSM_CONV_CH), row),
                   pl.BlockSpec((tm, LANES), row)],
        out_shape=[jax.ShapeDtypeStruct((t, SSM_INNER), F32), jax.ShapeDtypeStruct((t, SSM_CONV_CH), F32),
                   jax.ShapeDtypeStruct((t, LANES), F32)],
        compiler_params=_params("parallel"),
        name="odd_proj",
    )(x2d, gain.reshape(1, D_MODEL), wz, wx, wd)


def _ssd_kernel(z_ref, xbc_ref, dt_ref, cw_ref, cb_ref, dtb_ref, a_ref, dskip_ref, ng_ref, tri_ref, exp_ref,
                y_ref, st_ref, ext_ref, xc_ref):
    rows_blk = xbc_ref.shape[0]

    @pl.when(pl.program_id(1) == 0)
    def _():
        st_ref[...] = jnp.zeros_like(st_ref)
        ext_ref[0:SUBLANES, :] = jnp.zeros((SUBLANES, SSM_CONV_CH), F32)

    cur = xbc_ref[...]
    ext_ref[SUBLANES:, :] = cur
    acc = cb_ref[...] + cw_ref[SSM_CONV - 1:SSM_CONV, :] * cur
    for k in range(1, SSM_CONV):
        acc = acc + cw_ref[SSM_CONV - 1 - k:SSM_CONV - k, :] * ext_ref[SUBLANES - k:SUBLANES - k + rows_blk, :]
    ext_ref[0:SUBLANES, :] = cur[rows_blk - SUBLANES:]
    xc_ref[...] = _silu(acc)

    tri = tri_ref[...]
    expand = exp_ref[...]
    a_neg = -jnp.exp(a_ref[...])
    lane = lax.broadcasted_iota(jnp.int32, (CHUNK, LANES), 1)
    trow = lax.broadcasted_iota(jnp.int32, (CHUNK, LANES), 0)
    left = lane < SSM_HEAD_DIM
    diag = trow == jnp.where(left, lane, lane - SSM_HEAD_DIM)
    causal = trow >= jnp.where(left, lane, lane - SSM_HEAD_DIM)
    heads_per_group = SSM_HEADS // SSM_GROUPS
    b_off = SSM_INNER
    c_off = SSM_INNER + SSM_GROUPS * SSM_STATE

    def chunk(c, carry):
        r0 = pl.multiple_of(c * CHUNK, CHUNK)
        rows = pl.ds(r0, CHUNK)
        dt = dt_ref[rows, :] + dtb_ref[...]
        dt = jnp.maximum(dt, 0.0) + jnp.log(1.0 + jnp.exp(-jnp.abs(dt)))
        cum_h = _dot_f32(tri, dt * a_neg)
        c_hi = cum_h.astype(BF16)
        c_r = cum_h - c_hi.astype(F32)
        c_mid = c_r.astype(BF16)
        c_lo = (c_r - c_mid.astype(F32)).astype(BF16)
        cum = _dot(c_hi, expand) + _dot(c_mid, expand) + _dot(c_lo, expand)
        dtx = _dot(dt.astype(BF16), expand)
        xs = xc_ref[rows, 0:SSM_INNER]
        xdt = xs * dtx
        cum_last = cum[CHUNK - 1:CHUNK, :]
        decay_in = jnp.exp(cum)
        decay_out = jnp.exp(cum_last - cum)
        xw = (xdt * decay_out).astype(BF16)
        y_parts = []
        for g in range(SSM_GROUPS):
            gl = slice(g * SSM_GROUP_WIDTH, (g + 1) * SSM_GROUP_WIDTH)
            bm = xc_ref[rows, b_off + g * SSM_STATE:b_off + (g + 1) * SSM_STATE].astype(BF16)
            cm = xc_ref[rows, c_off + g * SSM_STATE:c_off + (g + 1) * SSM_STATE].astype(BF16)
            st = st_ref[g]
            y_g = _dot(cm, st.astype(BF16)) * decay_in[:, gl]
            st_ref[g] = st * jnp.exp(cum_last[:, gl]) + _dot_tn(bm, xw[:, gl])
            cb2 = _dot_nt(cm, jnp.concatenate([bm, bm], axis=0))
            pair_out = []
            for pr in range(heads_per_group // 2):
                sl = slice(g * SSM_GROUP_WIDTH + pr * LANES, g * SSM_GROUP_WIDTH + (pr + 1) * LANES)
                cum_p = cum[:, sl]
                cum_row = jnp.sum(jnp.where(diag, cum_p, 0.0), axis=0, keepdims=True)
                lmat = jnp.where(causal, jnp.exp(jnp.minimum(cum_p - cum_row, 0.0)), 0.0)
                xp = xdt[:, sl]
                x2 = jnp.concatenate([jnp.where(left, xp, 0.0), jnp.where(left, 0.0, xp)], axis=0).astype(BF16)
                pair_out.append(_dot((cb2 * lmat).astype(BF16), x2))
            y_parts.append(y_g + jnp.concatenate(pair_out, axis=1))
        y = jnp.concatenate(y_parts, axis=1) + dskip_ref[...] * xs
        y = y * _silu(z_ref[rows, :])
        outs = []
        for g in range(SSM_GROUPS):
            gl = slice(g * SSM_GROUP_WIDTH, (g + 1) * SSM_GROUP_WIDTH)
            outs.append(_rms_rows(y[:, gl], ng_ref[:, gl]))
        y_ref[rows, :] = jnp.concatenate(outs, axis=1).astype(y_ref.dtype)
        return carry

    lax.fori_loop(0, rows_blk // CHUNK, chunk, 0)


def _ssd(z, xbc, dt, conv_w, conv_b, dt_bias, a_log, d_skip, norm_g, batch, seq):
    t = z.shape[0]
    rb = SSD_BLOCK
    nblk = seq // rb
    tri = jnp.tril(jnp.ones((CHUNK, CHUNK), F32))
    lane = jnp.arange(SSM_INNER)
    expand = (jnp.arange(LANES)[:, None] == (lane[None, :] // SSM_HEAD_DIM)).astype(BF16)
    pad_h = lambda v: jnp.pad(v.astype(F32), (0, LANES - SSM_HEADS)).reshape(1, LANES)
    row = lambda b, s: (b * nblk + s, 0)
    fixed = lambda b, s: (0, 0)
    return pl.pallas_call(
        _ssd_kernel,
        grid=(batch, nblk),
        in_specs=[pl.BlockSpec((rb, SSM_INNER), row), pl.BlockSpec((rb, SSM_CONV_CH), row),
                  pl.BlockSpec((rb, LANES), row),
                  pl.BlockSpec((SSM_CONV, SSM_CONV_CH), fixed), pl.BlockSpec((1, SSM_CONV_CH), fixed),
                  pl.BlockSpec((1, LANES), fixed), pl.BlockSpec((1, LANES), fixed),
                  pl.BlockSpec((1, SSM_INNER), fixed), pl.BlockSpec((1, SSM_INNER), fixed),
                  pl.BlockSpec((CHUNK, CHUNK), fixed), pl.BlockSpec((LANES, SSM_INNER), fixed)],
        out_specs=pl.BlockSpec((rb, SSM_INNER), row),
        out_shape=jax.ShapeDtypeStruct((t, SSM_INNER), BF16),
        scratch_shapes=[pltpu.VMEM((SSM_GROUPS, SSM_STATE, SSM_GROUP_WIDTH), F32),
                        pltpu.VMEM((rb + SUBLANES, SSM_CONV_CH), F32), pltpu.VMEM((rb, SSM_CONV_CH), F32)],
        compiler_params=_params("parallel", "arbitrary"),
        name="ssd_scan",
    )(z, xbc, dt, conv_w.astype(F32), conv_b.reshape(1, SSM_CONV_CH).astype(F32), pad_h(dt_bias), pad_h(a_log),
      jnp.repeat(d_skip.astype(F32), SSM_HEAD_DIM).reshape(1, SSM_INNER), norm_g.reshape(1, SSM_INNER).astype(F32),
      tri, expand)


def kernel(x, mix_norm, ffn_norm, ev_w_in, hg_lb_logits, hg_out_norm, at_q_norm, at_k_norm, ev_w_out, od_w_in,
           od_conv_w, od_conv_b, od_dt_bias, od_A_log, od_D, od_out_norm, od_w_out, moe_wg, moe_bg, moe_we,
           moe_be, moe_w1, moe_w3, moe_w2):
    batch, seq, d = x.shape
    assert d == D_MODEL and seq % HG_BLOCK == 0 and seq % KEY_TILE == 0 and (batch * seq) % ROUTER_TILE == 0
    depth = mix_norm.shape[0]
    x2d = x.reshape(batch * seq, d).astype(F32)
    lbs = jnp.cumsum(jax.nn.softmax(hg_lb_logits.astype(F32), axis=0), axis=0)
    for layer in range(depth):
        i = layer // 2
        if layer % 2 == 0:
            hproj, q, k, v, iq, ik, iw = _even_proj(x2d, mix_norm[layer], ev_w_in[i], at_q_norm[i], at_k_norm[i],
                                                    seq)
            o_h = _hgrn(hproj, lbs[i], hg_out_norm[i], batch, seq)
            o_a = _dsa(q, iq, iw, k, v, ik, batch, seq)
            w_h = ev_w_out[i][:HG_WIDTH].astype(BF16)
            w_a = jnp.pad(ev_w_out[i][HG_WIDTH:].reshape(AT_HEADS, AT_DIM, d),
                          ((0, 0), (0, LANES - AT_DIM), (0, 0))).reshape(AT_HEADS * LANES, d).astype(BF16)
            acts, weights = [o_h, o_a], [w_h, w_a]
        else:
            z, xbc, dt = _odd_proj(x2d, mix_norm[layer], od_w_in[i])
            y = _ssd(z, xbc, dt, od_conv_w[i], od_conv_b[i], od_dt_bias[i], od_A_log[i], od_D[i], od_out_norm[i],
                     batch, seq)
            acts, weights = [y], [od_w_out[i].astype(BF16)]
        x1, hn, route, counts = _out_router(x2d, acts, weights, ffn_norm[layer], moe_wg[layer], moe_bg[layer],
                                            moe_we[layer], moe_be[layer])
        x2d = _moe(x1, hn, route, counts, moe_w1[layer], moe_w3[layer], moe_w2[layer])
    return x2d.reshape(batch, seq, d).astype(x.dtype)
```

```python
import functools
import math

import jax
import jax.numpy as jnp
from jax import lax
from jax.experimental import pallas as pl
from jax.experimental.pallas import tpu as pltpu

D_MODEL = 1024
CHUNK = 64
Q_BLOCK = 128
ROPE_THETA = 10000.0
EPS = 1e-6

HG_HEADS = 4
HG_DIM = 128
HG_WIDTH = HG_HEADS * HG_DIM
AT_HEADS = 8
AT_KV_HEADS = 2
AT_GROUP = AT_HEADS // AT_KV_HEADS
AT_DIM = 64
AT_WIDTH = AT_HEADS * AT_DIM
KV_WIDTH = AT_KV_HEADS * AT_DIM
IDX_HEADS = 4
IDX_DIM = 64
TOPK_MAX = 256

SSM_INNER = 2 * D_MODEL
SSM_HEAD_DIM = 64
SSM_HEADS = SSM_INNER // SSM_HEAD_DIM
SSM_GROUPS = 4
SSM_STATE = 128
SSM_CONV = 4
SSM_CONV_CH = SSM_INNER + 2 * SSM_GROUPS * SSM_STATE
SSM_GROUP_WIDTH = SSM_INNER // SSM_GROUPS

MOE_GROUPS = 4
MOE_PER_GROUP = 8
MOE_EXPERTS = MOE_GROUPS * MOE_PER_GROUP
MOE_FF = 512

LANES = 128
SUBLANES = 8
VMEM_LIMIT = 56 * 1024 * 1024
ROW_TILE = 256
ROUTER_TILE = 512
KEY_TILE = 512
HG_BLOCK = 512
SSD_BLOCK = 256
EXPERT_TILE = 512
ROUTE_ROWS = 8 + MOE_EXPERTS
LOG2E = math.log2(math.e)
INT_MIN = -(2 ** 31)
NEG_BIG = -1e30

F32 = jnp.float32
BF16 = jnp.bfloat16
HIGHEST = lax.Precision.HIGHEST


def _params(*sem):
    return pltpu.CompilerParams(dimension_semantics=sem, vmem_limit_bytes=VMEM_LIMIT)


def _dot(a, b):
    return jnp.dot(a, b, preferred_element_type=F32)


def _dot_f32(a, b):
    return jnp.dot(a, b, preferred_element_type=F32, precision=HIGHEST)


def _bf16_pieces(x):
    hi = x.astype(BF16)
    rest = x - hi.astype(F32)
    mid = rest.astype(BF16)
    return hi, mid, (rest - mid.astype(F32)).astype(BF16)


def _select_rows(sel, x):
    hi, mid, lo = _bf16_pieces(x)
    return _dot(sel, hi) + _dot(sel, mid) + _dot(sel, lo)


def _select_cols(x, sel):
    hi, mid, lo = _bf16_pieces(x)
    return _dot(hi, sel) + _dot(mid, sel) + _dot(lo, sel)


def _dot_nt(a, b, precision=None):
    return lax.dot_general(a, b, (((1,), (1,)), ((), ())), preferred_element_type=F32, precision=precision)


def _dot_tn(a, b, precision=None):
    return lax.dot_general(a, b, (((0,), (0,)), ((), ())), preferred_element_type=F32, precision=precision)


def _sigmoid(x):
    return 1.0 / (1.0 + jnp.exp(-x))


def _silu(x):
    return x * _sigmoid(x)


def _rms_rows(x, gain):
    return x * lax.rsqrt(jnp.mean(x * x, axis=-1, keepdims=True) + EPS) * gain


def _rope_slab(x, cos, sin_lo, sin_hi):
    return x * cos + pltpu.roll(x, LANES - 32, 1) * sin_lo + pltpu.roll(x, 32, 1) * sin_hi


def _even_proj_kernel(x_ref, g_ref, wh_ref, wq_ref, wkv_ref, wiq_ref, wik_ref,
                      r1_ref, gm_ref, qn_ref, kn_ref,
                      h_ref, q_ref, k_ref, v_ref, iq_ref, ik_ref, iw_ref):
    xn = _rms_rows(x_ref[...], g_ref[...]).astype(BF16)
    h_ref[...] = _dot(xn, wh_ref[...])

    cos1, lo1, hi1 = r1_ref[0], r1_ref[1], r1_ref[2]
    gm = gm_ref[...]

    def qk_norm(v, gain):
        ms = _dot_f32(v * v, gm)
        return v * lax.rsqrt(ms + EPS) * gain

    q = _dot(xn, wq_ref[...])
    for h in range(AT_HEADS):
        sl = slice(h * LANES, (h + 1) * LANES)
        qh = _rope_slab(qk_norm(q[:, sl], qn_ref[...]), cos1, lo1, hi1)
        q_ref[:, sl] = (qh * (AT_DIM ** -0.5 * LOG2E)).astype(BF16)
    kv = _dot(xn, wkv_ref[...])
    for h in range(AT_KV_HEADS):
        sl = slice(h * LANES, (h + 1) * LANES)
        k_ref[:, sl] = _rope_slab(qk_norm(kv[:, sl], kn_ref[...]), cos1, lo1, hi1).astype(BF16)
    v = kv[:, AT_KV_HEADS * LANES:]
    lane = lax.broadcasted_iota(jnp.int32, v.shape, 1)
    v_ref[...] = jnp.where(lane % LANES == AT_DIM, 1.0, v).astype(BF16)
    iq = _dot(xn, wiq_ref[...])
    for h in range(IDX_HEADS):
        sl = slice(h * LANES, (h + 1) * LANES)
        iq_ref[:, sl] = (_rope_slab(iq[:, sl], cos1, lo1, hi1) * (IDX_DIM ** -0.5)).astype(BF16)
    ikw = _rope_slab(_dot(xn, wik_ref[...]), cos1, lo1, hi1)
    ik_ref[...] = ikw.astype(BF16)
    iw_ref[...] = ikw


def _pad_heads(w, heads):
    d = w.shape[0]
    w = w.reshape(d, heads, 64)
    return jnp.pad(w, ((0, 0), (0, 0), (0, 64))).reshape(d, heads * LANES)


def _rope_tables(seq):
    half = AT_DIM // 2
    inv = jnp.exp(-math.log(ROPE_THETA) * jnp.arange(half, dtype=F32) / half)
    ang = jnp.arange(seq, dtype=F32)[:, None] * inv[None, :]
    c, s = jnp.cos(ang), jnp.sin(ang)
    z, o = jnp.zeros_like(c), jnp.ones_like(c)
    return jnp.stack([jnp.concatenate([c, c, o, o], 1), jnp.concatenate([-s, z, z, z], 1),
                      jnp.concatenate([z, s, z, z], 1)])


def _even_proj(x2d, gain, w_in, q_norm, k_norm, seq):
    t = x2d.shape[0]
    tm = ROW_TILE
    o = [0, 4 * HG_WIDTH]
    for wdt in (AT_WIDTH, KV_WIDTH, KV_WIDTH, IDX_HEADS * IDX_DIM, IDX_DIM, IDX_HEADS):
        o.append(o[-1] + wdt)
    wh = w_in[:, o[0]:o[1]].astype(BF16)
    wq = _pad_heads(w_in[:, o[1]:o[2]], AT_HEADS).astype(BF16)
    wkv = jnp.concatenate([_pad_heads(w_in[:, o[2]:o[3]], AT_KV_HEADS),
                           _pad_heads(w_in[:, o[3]:o[4]], AT_KV_HEADS)], 1).astype(BF16)
    wiq = _pad_heads(w_in[:, o[4]:o[5]], IDX_HEADS).astype(BF16)
    wik = jnp.pad(w_in[:, o[5]:o[7]], ((0, 0), (0, LANES - IDX_DIM - IDX_HEADS))).astype(BF16)
    rope1 = _rope_tables(seq)
    lane = jnp.arange(LANES)
    gm = jnp.where((lane[:, None] < AT_DIM) & (lane[None, :] < AT_DIM), 1.0 / AT_DIM, 0.0).astype(F32)
    pad_gain = lambda g: jnp.pad(g, (0, LANES - AT_DIM), constant_values=1.0).reshape(1, LANES)
    nseq = seq // tm
    row = lambda i: (i, 0)
    fixed = lambda i: (0, 0)
    rope_map = lambda i: (0, i % nseq, 0)
    outs = pl.pallas_call(
        _even_proj_kernel,
        grid=(t // tm,),
        in_specs=[pl.BlockSpec((tm, D_MODEL), row), pl.BlockSpec((1, D_MODEL), fixed),
                  pl.BlockSpec(wh.shape, fixed), pl.BlockSpec(wq.shape, fixed), pl.BlockSpec(wkv.shape, fixed),
                  pl.BlockSpec(wiq.shape, fixed), pl.BlockSpec(wik.shape, fixed),
                  pl.BlockSpec((3, tm, LANES), rope_map),
                  pl.BlockSpec((LANES, LANES), fixed), pl.BlockSpec((1, LANES), fixed),
                  pl.BlockSpec((1, LANES), fixed)],
        out_specs=[pl.BlockSpec((tm, 4 * HG_WIDTH), row), pl.BlockSpec((tm, AT_HEADS * LANES), row),
                   pl.BlockSpec((tm, AT_KV_HEADS * LANES), row), pl.BlockSpec((tm, AT_KV_HEADS * LANES), row),
                   pl.BlockSpec((tm, IDX_HEADS * LANES), row),
                   pl.BlockSpec((tm, LANES), row), pl.BlockSpec((tm, LANES), row)],
        out_shape=[jax.ShapeDtypeStruct((t, 4 * HG_WIDTH), F32), jax.ShapeDtypeStruct((t, AT_HEADS * LANES), BF16),
                   jax.ShapeDtypeStruct((t, AT_KV_HEADS * LANES), BF16),
                   jax.ShapeDtypeStruct((t, AT_KV_HEADS * LANES), BF16),
                   jax.ShapeDtypeStruct((t, IDX_HEADS * LANES), BF16),
                   jax.ShapeDtypeStruct((t, LANES), BF16), jax.ShapeDtypeStruct((t, LANES), F32)],
        compiler_params=_params("parallel"),
        name="even_proj",
    )(x2d, gain.reshape(1, D_MODEL), wh, wq, wkv, wiq, wik, rope1, gm, pad_gain(q_norm), pad_gain(k_norm))
    return outs


HG_SUB = 16
HG_SAFE_SPAN = 60.0


def _hgrn_kernel(h_ref, lb_ref, gn_ref, tri_ref, o_ref, st_ref, od_ref):
    @pl.when(pl.program_id(1) == 0)
    def _():
        st_ref[...] = jnp.zeros_like(st_ref)

    tri = tri_ref[...]
    s_idx = lax.broadcasted_iota(jnp.int32, (HG_SUB, 1), 0)

    t_idx = lax.broadcasted_iota(jnp.int32, (CHUNK, CHUNK), 0)
    c_idx = lax.broadcasted_iota(jnp.int32, (CHUNK, CHUNK), 1)

    def col(stream, h):
        return slice((stream * HG_HEADS + h) * HG_DIM, (stream * HG_HEADS + h + 1) * HG_DIM)

    def chunk(c, carry):
        r0 = pl.multiple_of(c * CHUNK, CHUNK)
        rows = pl.ds(r0, CHUNK)
        gates = []
        for h in range(HG_HEADS):
            lb = lb_ref[:, h * HG_DIM:(h + 1) * HG_DIM]
            hf = h_ref[rows, col(1, h)]
            sg = _sigmoid(hf)
            f = lb + (1.0 - lb) * sg
            kk = (1.0 - lb) * (1.0 - sg)
            gates.append((kk, _select_rows(tri, jnp.log(f))))
        floor = jnp.min(jnp.concatenate([b[CHUNK - 1:CHUNK] for _, b in gates], axis=1))

        @pl.when(floor >= -HG_SAFE_SPAN)
        def _():
            for h in range(HG_HEADS):
                head_chunk(rows, h, *gates[h], intra_split)

        @pl.when(floor < -HG_SAFE_SPAN)
        def _():
            for h in range(HG_HEADS):
                head_chunk(rows, h, *gates[h], intra_exact)

        return carry

    def intra_split(h, q, kk, v, b):
        del h
        a = _dot_nt((q * jnp.exp(b)).astype(BF16), (kk * jnp.exp(-b)).astype(BF16))
        return _dot(jnp.where(c_idx <= t_idx, a, 0.0).astype(BF16), v.astype(BF16))

    def head_chunk(rows, h, kk, b, intra):
        sl = slice(h * HG_DIM, (h + 1) * HG_DIM)
        q = h_ref[rows, col(0, h)]
        v = h_ref[rows, col(2, h)]
        st = st_ref[h]
        o = _dot_nt((q * jnp.exp(b)).astype(BF16), st.astype(BF16)) + intra(h, q, kk, v, b)
        b_last = b[CHUNK - 1:CHUNK, :]
        st_ref[h] = st * jnp.exp(b_last) + _dot_tn(v.astype(BF16), (kk * jnp.exp(b_last - b)).astype(BF16))
        o = _rms_rows(o, gn_ref[:, sl]) * _silu(h_ref[rows, col(3, h)])
        o_ref[rows, sl] = o.astype(o_ref.dtype)

    def intra_exact(h, q, kk, v, b):
        od = od_ref.at[h]
        off = []
        for blk in range(CHUNK // HG_SUB):
            lo, hi = blk * HG_SUB, (blk + 1) * HG_SUB
            bs, ks, vs = b[lo:hi], kk[lo:hi], v[lo:hi]
            for tt in range(HG_SUB):
                t = lo + tt
                e = jnp.exp(jnp.minimum(b[t:t + 1] - bs, 0.0))
                a = jnp.sum(e * (q[t:t + 1] * ks), axis=-1, keepdims=True)
                a = jnp.where(s_idx <= tt, a, 0.0)
                od[t:t + 1, :] = jnp.sum(a * vs, axis=0, keepdims=True)
            if blk == 0:
                off.append(jnp.zeros((HG_SUB, HG_DIM), F32))
            else:
                ref = b[lo - 1:lo]
                qd = (q[lo:hi] * jnp.exp(b[lo:hi] - ref)).astype(BF16)
                kd = (kk[:lo] * jnp.exp(ref - b[:lo])).astype(BF16)
                off.append(_dot(_dot_nt(qd, kd).astype(BF16), v[:lo].astype(BF16)))
        return od[...] + jnp.concatenate(off, axis=0)

    lax.fori_loop(0, HG_BLOCK // CHUNK, chunk, 0)


def _hgrn(hproj, lb, out_norm, batch, seq):
    t = hproj.shape[0]
    nblk = seq // HG_BLOCK
    tri = jnp.tril(jnp.ones((CHUNK, CHUNK), BF16))
    row = lambda b, s: (b * nblk + s, 0)
    fixed = lambda b, s: (0, 0)
    return pl.pallas_call(
        _hgrn_kernel,
        grid=(batch, nblk),
        in_specs=[pl.BlockSpec((HG_BLOCK, 4 * HG_WIDTH), row), pl.BlockSpec((1, HG_WIDTH), fixed),
                  pl.BlockSpec((1, HG_WIDTH), fixed), pl.BlockSpec((CHUNK, CHUNK), fixed)],
        out_specs=pl.BlockSpec((HG_BLOCK, HG_WIDTH), row),
        out_shape=jax.ShapeDtypeStruct((t, HG_WIDTH), BF16),
        scratch_shapes=[pltpu.VMEM((HG_HEADS, HG_DIM, HG_DIM), F32), pltpu.VMEM((HG_HEADS, CHUNK, HG_DIM), F32)],
        compiler_params=_params("parallel", "arbitrary"),
        name="hgrn2_scan",
    )(hproj, lb.reshape(1, HG_WIDTH), out_norm.reshape(1, HG_WIDTH), tri)


COUNT_LANES = 4
SEARCH_UNTESTED_BITS = 18


def _dsa_kernel(q_ref, iq_ref, iw_ref, k_ref, vt_ref, ik_ref, tril_ref, o_ref, keys_ref, *, topk):
    j = pl.program_id(1)
    kb = KEY_TILE
    nkt = (j * Q_BLOCK + Q_BLOCK + kb - 1) // kb
    qrow = lax.broadcasted_iota(jnp.int32, (1, Q_BLOCK), 1)
    nadm = ((j * Q_BLOCK + qrow) // CHUNK + 1) * CHUNK
    key_row = lax.broadcasted_iota(jnp.int32, (kb, Q_BLOCK), 0)
    iwt = jnp.transpose(iw_ref[...])
    iws = [iwt[IDX_DIM + h:IDX_DIM + h + 1, :] for h in range(IDX_HEADS)]

    iq_pairs = [jnp.concatenate([iq_ref[:, (2 * p + u) * LANES:(2 * p + u + 1) * LANES] for u in range(2)], axis=0)
                for p in range(IDX_HEADS // 2)]

    def score_tile(kt, carry):
        ik = ik_ref[kt]
        score = jnp.zeros((kb, Q_BLOCK), F32)
        for p in range(IDX_HEADS // 2):
            sc = _dot_nt(ik, iq_pairs[p])
            for u in range(2):
                score = score + iws[2 * p + u] * jnp.maximum(sc[:, u * Q_BLOCK:(u + 1) * Q_BLOCK], 0.0)
        bits = pltpu.bitcast(score, jnp.int32)
        key = jnp.where(bits < 0, bits ^ 0x7FFFFFFF, bits)
        key = jnp.where(bits == INT_MIN, 0, key)
        keys_ref[kt] = jnp.where(kt * kb + key_row < nadm, key, INT_MIN)
        return carry

    lax.fori_loop(0, nkt, score_tile, 0)

    @pl.when(nkt % 2 == 1)
    def _():
        keys_ref[nkt] = jnp.full((kb, Q_BLOCK), INT_MIN, jnp.int32)

    def fold(ind):
        return jnp.sum(ind.reshape(COUNT_LANES, kb // (COUNT_LANES * SUBLANES), SUBLANES, Q_BLOCK), axis=1)

    def count(indicator):
        def body(p, acc):
            return acc + fold(indicator(keys_ref[2 * p])) + fold(indicator(keys_ref[2 * p + 1]))
        acc = lax.fori_loop(0, (nkt + 1) // 2, body, jnp.zeros((COUNT_LANES, SUBLANES, Q_BLOCK), F32))
        return jnp.sum(jnp.sum(acc, axis=0), axis=0, keepdims=True)

    zero_ge = count(lambda ks: jnp.where(ks >= 0, 1.0, 0.0))
    zero_gt = count(lambda ks: jnp.where(ks > 0, 1.0, 0.0))
    at_zero = jnp.where(zero_gt < topk, jnp.where(zero_ge >= topk, 1.0, 0.0), 0.0)
    settled0 = jnp.where(nadm < topk, 1.0, at_zero)
    found0 = jnp.where(nadm < topk, INT_MIN, 0)

    def search_bit(i, st):
        tu, found, settled = st
        cand = tu | jnp.left_shift(jnp.int32(1), 31 - i)
        cand_s = cand ^ INT_MIN
        cnt = count(lambda ks: jnp.where(ks >= cand_s, 1.0, 0.0))
        hit = jnp.where(settled > 0.5, 0.0, jnp.where(cnt == topk, 1.0, 0.0))
        found = jnp.where(hit > 0.5, cand_s, found)
        settled = jnp.maximum(settled, hit)
        tu = jnp.where(cnt >= topk, cand, tu)
        return tu, found, settled

    tu0 = jnp.where(zero_ge >= topk, INT_MIN, 0)
    st = lax.fori_loop(1, SEARCH_UNTESTED_BITS, search_bit, (tu0, found0, settled0))

    def search_cond(c):
        i, _, open_rows = c
        return jnp.logical_and(i < 32, open_rows > 0.5)

    def search_step(c):
        i, st, _ = c
        st = search_bit(i, st)
        return i + 1, st, jnp.max(1.0 - st[2])

    _, (tu, found, settled), _ = lax.while_loop(
        search_cond, search_step, (jnp.int32(SEARCH_UNTESTED_BITS), st, jnp.max(1.0 - st[2])))
    thr = jnp.where(settled > 0.5, found, tu ^ INT_MIN)
    need = topk - count(lambda ks: jnp.where(ks > thr, 1.0, 0.0))

    rows4 = AT_GROUP * Q_BLOCK
    qs = [jnp.concatenate([q_ref[:, (g * AT_GROUP + hh) * LANES:(g * AT_GROUP + hh + 1) * LANES]
                           for hh in range(AT_GROUP)], axis=0) for g in range(AT_KV_HEADS)]

    def masked_scores(kt, seen):
        ks = keys_ref[kt]
        tied = jnp.where(ks == thr, 1.0, 0.0).astype(BF16)
        slabs = []
        for sb in range(kb // LANES):
            seen = _dot(tril_ref[...], tied[sb * LANES:(sb + 1) * LANES]) + seen[-1:]
            slabs.append(seen)
        rank = jnp.concatenate(slabs, axis=0)
        seen = seen[-1:]
        tie = jnp.where(ks == thr, jnp.where(rank <= need, 0.0, NEG_BIG), NEG_BIG)
        bias = jnp.where(kt * kb + key_row < nadm, jnp.where(ks > thr, 0.0, tie), NEG_BIG)
        bias4 = jnp.concatenate([bias] * AT_GROUP, axis=1)
        k_tile = k_ref[kt]
        return seen, tuple(_dot_nt(k_tile[:, g * LANES:(g + 1) * LANES], qs[g]) + bias4
                           for g in range(AT_KV_HEADS))

    def att_tile(kt, carry):
        seen, state = carry
        seen, scores = masked_scores(kt, seen)
        vt_tile = vt_ref[kt]
        out = []
        for g in range(AT_KV_HEADS):
            m, acc = state[g]
            s = scores[g]
            m_new = jnp.maximum(m, jnp.max(s, axis=0, keepdims=True))
            alpha = jnp.exp2(m - m_new)
            p = jnp.exp2(s - m_new).astype(BF16)
            acc = alpha * acc + _dot(vt_tile[g * LANES:(g + 1) * LANES, :], p)
            out.append((m_new, acc))
        return seen, tuple(out)

    init = tuple((jnp.full((1, rows4), 0.1 * NEG_BIG, F32), jnp.zeros((LANES, rows4), F32))
                 for _ in range(AT_KV_HEADS))
    _, res = lax.fori_loop(0, (nkt + 1) // 2, lambda p, c: att_tile(2 * p + 1, att_tile(2 * p, c)),
                           (jnp.zeros((1, Q_BLOCK), F32), init))
    for g in range(AT_KV_HEADS):
        _, acc = res[g]
        og = acc / acc[AT_DIM:AT_DIM + 1, :]
        for hh in range(AT_GROUP):
            h = g * AT_GROUP + hh
            o_ref[:, h * LANES:(h + 1) * LANES] = jnp.transpose(og[:, hh * Q_BLOCK:(hh + 1) * Q_BLOCK]).astype(o_ref.dtype)


def _dsa(q, iq, iw, k, v, ik, batch, seq):
    t = q.shape[0]
    nb = seq // Q_BLOCK
    nkt = seq // KEY_TILE
    topk = min(TOPK_MAX, seq // 4)
    width = AT_KV_HEADS * LANES
    k3 = k.reshape(batch * nkt, KEY_TILE, width)
    vt3 = jnp.swapaxes(v.reshape(batch * nkt, KEY_TILE, width), 1, 2)
    ik3 = ik.reshape(batch * nkt, KEY_TILE, LANES)
    ki = jnp.arange(LANES)
    tril = (ki[None, :] <= ki[:, None]).astype(BF16)
    qrow = lambda b, j: (b * nb + j, 0)
    seq_blk = lambda b, j: (b, 0, 0)
    return pl.pallas_call(
        functools.partial(_dsa_kernel, topk=topk),
        grid=(batch, nb),
        in_specs=[pl.BlockSpec((Q_BLOCK, AT_HEADS * LANES), qrow), pl.BlockSpec((Q_BLOCK, IDX_HEADS * LANES), qrow),
                  pl.BlockSpec((Q_BLOCK, LANES), qrow),
                  pl.BlockSpec((nkt, KEY_TILE, width), seq_blk), pl.BlockSpec((nkt, width, KEY_TILE), seq_blk),
                  pl.BlockSpec((nkt, KEY_TILE, LANES), seq_blk),
                  pl.BlockSpec((LANES, LANES), lambda b, j: (0, 0))],
        out_specs=pl.BlockSpec((Q_BLOCK, AT_HEADS * LANES), qrow),
        out_shape=jax.ShapeDtypeStruct((t, AT_HEADS * LANES), BF16),
        scratch_shapes=[pltpu.VMEM((nkt + 1, KEY_TILE, Q_BLOCK), jnp.int32)],
        compiler_params=_params("parallel", "arbitrary"),
        name="dsa_attention",
    )(q, iq, iw, k3, vt3, ik3, tril)


def _out_router_kernel(*refs, n_in):
    x_ref = refs[0]
    a_refs = refs[1:1 + n_in]
    w_refs = refs[1 + n_in:1 + 2 * n_in]
    g_ref, wr_ref, br_ref, su_ref = refs[1 + 2 * n_in:5 + 2 * n_in]
    x1_ref, hn_ref, route_ref, cnt_ref = refs[5 + 2 * n_in:9 + 2 * n_in]
    carry_ref = refs[9 + 2 * n_in]

    @pl.when(pl.program_id(0) == 0)
    def _():
        carry_ref[...] = jnp.zeros_like(carry_ref)

    x1 = x_ref[...]
    for a_ref, w_ref in zip(a_refs, w_refs):
        x1 = x1 + _dot(a_ref[...], w_ref[...])
    x1_ref[...] = x1
    hn = _rms_rows(x1, g_ref[...])
    hn_ref[...] = hn

    lt = _dot_nt(wr_ref[...], hn, precision=HIGHEST) + br_ref[...]
    tm = lt.shape[1]
    r4 = lax.broadcasted_iota(jnp.int32, (MOE_GROUPS, tm), 0).astype(F32)
    gl = lt[0:MOE_GROUPS]
    gmax = jnp.max(gl, axis=0, keepdims=True)
    gidx = jnp.min(jnp.where(gl == gmax, r4, float(MOE_GROUPS)), axis=0, keepdims=True)
    gval = 1.0 / jnp.sum(jnp.exp(gl - gmax), axis=0, keepdims=True)
    el = jnp.zeros((MOE_PER_GROUP, tm), F32)
    for g in range(MOE_GROUPS):
        el = el + jnp.where(gidx == float(g), lt[8 + g * MOE_PER_GROUP:8 + (g + 1) * MOE_PER_GROUP], 0.0)
    r8 = lax.broadcasted_iota(jnp.int32, (MOE_PER_GROUP, tm), 0).astype(F32)
    m1 = jnp.max(el, axis=0, keepdims=True)
    i1 = jnp.min(jnp.where(el == m1, r8, float(MOE_PER_GROUP)), axis=0, keepdims=True)
    el2 = jnp.where(r8 == i1, -jnp.inf, el)
    m2 = jnp.max(el2, axis=0, keepdims=True)
    i2 = jnp.min(jnp.where(el2 == m2, r8, float(MOE_PER_GROUP)), axis=0, keepdims=True)
    e21 = jnp.exp(m2 - m1)
    wa = gval / (1.0 + e21)
    wb = gval * e21 / (1.0 + e21)
    fa = gidx * MOE_PER_GROUP + i1
    fb = gidx * MOE_PER_GROUP + i2
    re = lax.broadcasted_iota(jnp.int32, (MOE_EXPERTS, tm), 0).astype(F32)
    hit_a = re == fa
    hit_b = re == fb
    member = jnp.where(hit_a, 1.0, jnp.where(hit_b, 1.0, 0.0))
    before = _dot(member.astype(BF16), su_ref[...]) + carry_ref[...]
    rank_a = jnp.sum(jnp.where(hit_a, before, 0.0), axis=0, keepdims=True)
    rank_b = jnp.sum(jnp.where(hit_b, before, 0.0), axis=0, keepdims=True)
    carry_ref[...] = carry_ref[...] + jnp.sum(member, axis=1, keepdims=True)
    cnt_ref[...] = jnp.broadcast_to(carry_ref[...], cnt_ref.shape)
    zero = jnp.zeros_like(wa)
    route_ref[...] = jnp.concatenate([fa, fb, wa, wb, rank_a, rank_b, zero, zero], axis=0)


def _out_router(x2d, acts, weights, ffn_gain, wg, bg, we, be):
    t = x2d.shape[0]
    tm = ROUTER_TILE
    n_in = len(acts)
    wr = jnp.concatenate([wg.T, jnp.zeros((8 - MOE_GROUPS, D_MODEL), F32), we.T], axis=0)
    br = jnp.concatenate([bg, jnp.zeros((8 - MOE_GROUPS,), F32), be]).reshape(ROUTE_ROWS, 1).astype(F32)
    ti = jnp.arange(tm)
    su = (ti[:, None] < ti[None, :]).astype(BF16)
    row = lambda i: (i, 0)
    fixed = lambda i: (0, 0)
    in_specs = [pl.BlockSpec((tm, D_MODEL), row)]
    in_specs += [pl.BlockSpec((tm, a.shape[1]), row) for a in acts]
    in_specs += [pl.BlockSpec(w.shape, fixed) for w in weights]
    in_specs += [pl.BlockSpec((1, D_MODEL), fixed), pl.BlockSpec(wr.shape, fixed), pl.BlockSpec(br.shape, fixed),
                 pl.BlockSpec(su.shape, fixed)]
    x1, hn, route, cnt = pl.pallas_call(
        functools.partial(_out_router_kernel, n_in=n_in),
        grid=(t // tm,),
        in_specs=in_specs,
        out_specs=[pl.BlockSpec((tm, D_MODEL), row), pl.BlockSpec((tm, D_MODEL), row),
                   pl.BlockSpec((8, tm), lambda i: (0, i)), pl.BlockSpec((MOE_EXPERTS, LANES), fixed)],
        out_shape=[jax.ShapeDtypeStruct((t, D_MODEL), F32), jax.ShapeDtypeStruct((t, D_MODEL), F32),
                   jax.ShapeDtypeStruct((8, t), F32), jax.ShapeDtypeStruct((MOE_EXPERTS, LANES), F32)],
        scratch_shapes=[pltpu.VMEM((MOE_EXPERTS, 1), F32)],
        compiler_params=_params("arbitrary"),
        name="out_proj_router",
    )(x2d, *acts, *weights, ffn_gain.reshape(1, D_MODEL), wr.astype(F32), br, su)
    return x1, hn, route, cnt[:, 0]


def _row_copy(src_ref, src_row, dst_ref, dst_row, sem):
    return pltpu.make_async_copy(src_ref.at[pl.ds(src_row, 1), :], dst_ref.at[pl.ds(dst_row, 1), :], sem)


ROWS_PER_ISSUE = 8


def _for_rows(n, fn):
    def body(i, c):
        for u in range(ROWS_PER_ISSUE):
            fn(i * ROWS_PER_ISSUE + u)
        return c

    lax.fori_loop(0, n // ROWS_PER_ISSUE, body, 0)


def _dispatch_kernel(sa_ref, sb_ref, hn_ref, xs_in_ref, xs_ref, sem):
    del xs_in_ref
    tm = hn_ref.shape[0]

    def copies(r):
        return (_row_copy(hn_ref, r, xs_ref, sa_ref[0, 0, r], sem), _row_copy(hn_ref, r, xs_ref, sb_ref[0, 0, r], sem))

    def start(r):
        for cp in copies(r):
            cp.start()

    def wait(r):
        for cp in copies(r):
            cp.wait()

    _for_rows(tm, start)
    _for_rows(tm, wait)


def _dispatch(hn, slot_a, slot_b, n_slots):
    t = hn.shape[0]
    tm = ROW_TILE
    smem = lambda: pl.BlockSpec((1, 1, tm), lambda i: (i, 0, 0), memory_space=pltpu.SMEM)
    xs0 = jnp.zeros((n_slots, D_MODEL), hn.dtype)
    return pl.pallas_call(
        _dispatch_kernel,
        grid=(t // tm,),
        in_specs=[smem(), smem(), pl.BlockSpec((tm, D_MODEL), lambda i: (i, 0)), pl.BlockSpec(memory_space=pl.ANY)],
        out_specs=pl.BlockSpec(memory_space=pl.ANY),
        out_shape=jax.ShapeDtypeStruct((n_slots, D_MODEL), hn.dtype),
        scratch_shapes=[pltpu.SemaphoreType.DMA(())],
        input_output_aliases={3: 0},
        compiler_params=_params("arbitrary"),
        name="moe_dispatch",
    )(slot_a.reshape(t // tm, 1, tm), slot_b.reshape(t // tm, 1, tm), hn, xs0)


def _ffn_kernel(te_ref, nv_ref, xs_ref, w1_ref, w3_ref, w2_ref, ys_ref, b1_ref, b3_ref, b2_ref):
    i = pl.program_id(0)

    @pl.when(jnp.logical_or(i == 0, te_ref[i] != te_ref[jnp.maximum(i - 1, 0)]))
    def _():
        b1_ref[...] = w1_ref[...].astype(BF16)
        b3_ref[...] = w3_ref[...].astype(BF16)
        b2_ref[...] = w2_ref[...].astype(BF16)

    @pl.when(i < nv_ref[0])
    def _():
        x = xs_ref[...].astype(BF16)
        a = _dot(x, b1_ref[...])
        b = _dot(x, b3_ref[...])
        ys_ref[...] = _dot((_silu(a) * b).astype(BF16), b2_ref[...])

    @pl.when(pl.program_id(0) >= nv_ref[0])
    def _():
        ys_ref[...] = jnp.zeros_like(ys_ref)


def _expert_ffn(xs, tile_expert, n_valid, w1, w3, w2):
    n_slots = xs.shape[0]
    n_tiles = n_slots // EXPERT_TILE
    wmap = lambda i, te, nv: (te[i], 0, 0)
    grid_spec = pltpu.PrefetchScalarGridSpec(
        num_scalar_prefetch=2,
        grid=(n_tiles,),
        in_specs=[pl.BlockSpec((EXPERT_TILE, D_MODEL), lambda i, te, nv: (i, 0)),
                  pl.BlockSpec((None, D_MODEL, MOE_FF), wmap), pl.BlockSpec((None, D_MODEL, MOE_FF), wmap),
                  pl.BlockSpec((None, MOE_FF, D_MODEL), wmap)],
        out_specs=pl.BlockSpec((EXPERT_TILE, D_MODEL), lambda i, te, nv: (i, 0)),
        scratch_shapes=[pltpu.VMEM((D_MODEL, MOE_FF), BF16), pltpu.VMEM((D_MODEL, MOE_FF), BF16),
                        pltpu.VMEM((MOE_FF, D_MODEL), BF16)],
    )
    return pl.pallas_call(
        _ffn_kernel,
        grid_spec=grid_spec,
        out_shape=jax.ShapeDtypeStruct((n_slots, D_MODEL), F32),
        compiler_params=_params("arbitrary"),
        name="moe_expert_ffn",
    )(tile_expert, n_valid, xs, w1, w3, w2)


def _combine_kernel(sa_ref, sb_ref, x_ref, wt_ref, ys_ref, o_ref, ya_ref, yb_ref, sem):
    tm = x_ref.shape[0]

    def copies(r):
        return (_row_copy(ys_ref, sa_ref[0, 0, r], ya_ref, r, sem), _row_copy(ys_ref, sb_ref[0, 0, r], yb_ref, r, sem))

    def start(r):
        for cp in copies(r):
            cp.start()

    def wait(r):
        for cp in copies(r):
            cp.wait()

    _for_rows(tm, start)
    _for_rows(tm, wait)
    wt = wt_ref[...]
    o_ref[...] = x_ref[...] + wt[:, 2:3] * ya_ref[...] + wt[:, 3:4] * yb_ref[...]


def _combine(x1, ys, slot_a, slot_b, wts):
    t = x1.shape[0]
    tm = ROW_TILE
    smem = lambda: pl.BlockSpec((1, 1, tm), lambda i: (i, 0, 0), memory_space=pltpu.SMEM)
    row = lambda i: (i, 0)
    return pl.pallas_call(
        _combine_kernel,
        grid=(t // tm,),
        in_specs=[smem(), smem(), pl.BlockSpec((tm, D_MODEL), row), pl.BlockSpec((tm, 8), row),
                  pl.BlockSpec(memory_space=pl.ANY)],
        out_specs=pl.BlockSpec((tm, D_MODEL), row),
        out_shape=jax.ShapeDtypeStruct((t, D_MODEL), F32),
        scratch_shapes=[pltpu.VMEM((tm, D_MODEL), F32), pltpu.VMEM((tm, D_MODEL), F32),
                        pltpu.SemaphoreType.DMA(())],
        compiler_params=_params("arbitrary"),
        name="moe_combine",
    )(slot_a.reshape(t // tm, 1, tm), slot_b.reshape(t // tm, 1, tm), x1, wts, ys)


def _moe(x1, hn, route, counts, w1, w3, w2, layer):
    t = x1.shape[0]
    n_tiles = (2 * t) // EXPERT_TILE + MOE_EXPERTS
    n_slots = n_tiles * EXPERT_TILE
    counts = counts.astype(jnp.int32)
    tiles_per = (counts + EXPERT_TILE - 1) // EXPERT_TILE
    ei = jnp.arange(MOE_EXPERTS, dtype=jnp.int32)
    tile_end = jnp.sum(jnp.where(ei[None, :] <= ei[:, None], tiles_per[None, :], 0), axis=1)
    seg_start = (tile_end - tiles_per) * EXPERT_TILE
    n_valid = tile_end[-1:]
    tile_idx = jnp.arange(n_tiles, dtype=jnp.int32)
    tile_expert = jnp.minimum(jnp.sum((tile_idx[:, None] >= tile_end[None, :]).astype(jnp.int32), axis=1),
                              MOE_EXPERTS - 1)
    ri = route.astype(jnp.int32)
    start_of = lambda e: jnp.sum(jnp.where(e[:, None] == ei[None, :], seg_start[None, :], 0), axis=1)
    slot_a = start_of(ri[0]) + ri[4]
    slot_b = start_of(ri[1]) + ri[5]
    xs = _dispatch(hn, slot_a, slot_b, n_slots)
    ys = _expert_ffn(xs, tile_expert + layer * MOE_EXPERTS, n_valid, w1.reshape(-1, D_MODEL, MOE_FF),
                     w3.reshape(-1, D_MODEL, MOE_FF), w2.reshape(-1, MOE_FF, D_MODEL))
    return _combine(x1, ys, slot_a, slot_b, jnp.transpose(route))


def _odd_proj_kernel(x_ref, g_ref, wz_ref, wx_ref, wd_ref, z_ref, xbc_ref, dt_ref):
    xn = _rms_rows(x_ref[...], g_ref[...]).astype(BF16)
    z_ref[...] = _dot(xn, wz_ref[...])
    xbc_ref[...] = _dot(xn, wx_ref[...])
    dt_ref[...] = _dot(xn, wd_ref[...])


def _odd_proj(x2d, gain, w_in):
    t = x2d.shape[0]
    tm = ROW_TILE
    wz = w_in[:, :SSM_INNER].astype(BF16)
    wx = w_in[:, SSM_INNER:SSM_INNER + SSM_CONV_CH].astype(BF16)
    wd = jnp.pad(w_in[:, SSM_INNER + SSM_CONV_CH:], ((0, 0), (0, LANES - SSM_HEADS))).astype(BF16)
    row = lambda i: (i, 0)
    fixed = lambda i: (0, 0)
    return pl.pallas_call(
        _odd_proj_kernel,
        grid=(t // tm,),
        in_specs=[pl.BlockSpec((tm, D_MODEL), row), pl.BlockSpec((1, D_MODEL), fixed),
                  pl.BlockSpec(wz.shape, fixed), pl.BlockSpec(wx.shape, fixed), pl.BlockSpec(wd.shape, fixed)],
        out_specs=[pl.BlockSpec((tm, SSM_INNER), row), pl.BlockSpec((tm, SSM_CONV_CH), row),
                   pl.BlockSpec((tm, LANES), row)],
        out_shape=[jax.ShapeDtypeStruct((t, SSM_INNER), F32), jax.ShapeDtypeStruct((t, SSM_CONV_CH), F32),
                   jax.ShapeDtypeStruct((t, LANES), F32)],
        compiler_params=_params("parallel"),
        name="odd_proj",
    )(x2d, gain.reshape(1, D_MODEL), wz, wx, wd)


def _ssd_kernel(z_ref, xbc_ref, dt_ref, cw_ref, cb_ref, dtb_ref, a_ref, dskip_ref, ng_ref, tri_ref, exp_ref,
                y_ref, st_ref, ext_ref, xc_ref):
    rows_blk = xbc_ref.shape[0]

    @pl.when(pl.program_id(1) == 0)
    def _():
        st_ref[...] = jnp.zeros_like(st_ref)
        ext_ref[0:SUBLANES, :] = jnp.zeros((SUBLANES, SSM_CONV_CH), F32)

    cur = xbc_ref[...]
    ext_ref[SUBLANES:, :] = cur
    acc = cb_ref[...] + cw_ref[SSM_CONV - 1:SSM_CONV, :] * cur
    for k in range(1, SSM_CONV):
        acc = acc + cw_ref[SSM_CONV - 1 - k:SSM_CONV - k, :] * ext_ref[SUBLANES - k:SUBLANES - k + rows_blk, :]
    ext_ref[0:SUBLANES, :] = cur[rows_blk - SUBLANES:]
    xc_ref[...] = _silu(acc)

    tri = tri_ref[...]
    expand = exp_ref[...]
    a_neg = -jnp.exp(a_ref[...])
    lane = lax.broadcasted_iota(jnp.int32, (CHUNK, LANES), 1)
    trow = lax.broadcasted_iota(jnp.int32, (CHUNK, LANES), 0)
    left = lane < SSM_HEAD_DIM
    diag = trow == jnp.where(left, lane, lane - SSM_HEAD_DIM)
    causal = trow >= jnp.where(left, lane, lane - SSM_HEAD_DIM)
    heads_per_group = SSM_HEADS // SSM_GROUPS
    b_off = SSM_INNER
    c_off = SSM_INNER + SSM_GROUPS * SSM_STATE

    def chunk(c, carry):
        r0 = pl.multiple_of(c * CHUNK, CHUNK)
        rows = pl.ds(r0, CHUNK)
        dt = dt_ref[rows, :] + dtb_ref[...]
        dt = jnp.maximum(dt, 0.0) + jnp.log(1.0 + jnp.exp(-jnp.abs(dt)))
        cum = _select_cols(_select_rows(tri, dt * a_neg), expand)
        dtx = _dot(dt.astype(BF16), expand)
        xs = xc_ref[rows, 0:SSM_INNER]
        xdt = xs * dtx
        cum_last = cum[CHUNK - 1:CHUNK, :]
        decay_in = jnp.exp(cum)
        decay_out = jnp.exp(cum_last - cum)
        xw = (xdt * decay_out).astype(BF16)
        y_parts = []
        for g in range(SSM_GROUPS):
            gl = slice(g * SSM_GROUP_WIDTH, (g + 1) * SSM_GROUP_WIDTH)
            bm = xc_ref[rows, b_off + g * SSM_STATE:b_off + (g + 1) * SSM_STATE].astype(BF16)
            cm = xc_ref[rows, c_off + g * SSM_STATE:c_off + (g + 1) * SSM_STATE].astype(BF16)
            st = st_ref[g]
            y_g = _dot(cm, st.astype(BF16)) * decay_in[:, gl]
            st_ref[g] = st * jnp.exp(cum_last[:, gl]) + _dot_tn(bm, xw[:, gl])
            cb2 = _dot_nt(cm, jnp.concatenate([bm, bm], axis=0))
            pair_out = []
            for pr in range(heads_per_group // 2):
                sl = slice(g * SSM_GROUP_WIDTH + pr * LANES, g * SSM_GROUP_WIDTH + (pr + 1) * LANES)
                cum_p = cum[:, sl]
                cum_row = jnp.sum(jnp.where(diag, cum_p, 0.0), axis=0, keepdims=True)
                lmat = jnp.where(causal, jnp.exp(jnp.minimum(cum_p - cum_row, 0.0)), 0.0)
                xp = xdt[:, sl]
                x2 = jnp.concatenate([jnp.where(left, xp, 0.0), jnp.where(left, 0.0, xp)], axis=0).astype(BF16)
                pair_out.append(_dot((cb2 * lmat).astype(BF16), x2))
            y_parts.append(y_g + jnp.concatenate(pair_out, axis=1))
        y = jnp.concatenate(y_parts, axis=1) + dskip_ref[...] * xs
        y = y * _silu(z_ref[rows, :])
        outs = []
        for g in range(SSM_GROUPS):
            gl = slice(g * SSM_GROUP_WIDTH, (g + 1) * SSM_GROUP_WIDTH)
            outs.append(_rms_rows(y[:, gl], ng_ref[:, gl]))
        y_ref[rows, :] = jnp.concatenate(outs, axis=1).astype(y_ref.dtype)
        return carry

    lax.fori_loop(0, rows_blk // CHUNK, chunk, 0, unroll=2)


def _ssd(z, xbc, dt, conv_w, conv_b, dt_bias, a_log, d_skip, norm_g, batch, seq):
    t = z.shape[0]
    rb = SSD_BLOCK
    nblk = seq // rb
    tri = jnp.tril(jnp.ones((CHUNK, CHUNK), BF16))
    lane = jnp.arange(SSM_INNER)
    expand = (jnp.arange(LANES)[:, None] == (lane[None, :] // SSM_HEAD_DIM)).astype(BF16)
    pad_h = lambda v: jnp.pad(v.astype(F32), (0, LANES - SSM_HEADS)).reshape(1, LANES)
    row = lambda b, s: (b * nblk + s, 0)
    fixed = lambda b, s: (0, 0)
    return pl.pallas_call(
        _ssd_kernel,
        grid=(batch, nblk),
        in_specs=[pl.BlockSpec((rb, SSM_INNER), row), pl.BlockSpec((rb, SSM_CONV_CH), row),
                  pl.BlockSpec((rb, LANES), row),
                  pl.BlockSpec((SSM_CONV, SSM_CONV_CH), fixed), pl.BlockSpec((1, SSM_CONV_CH), fixed),
                  pl.BlockSpec((1, LANES), fixed), pl.BlockSpec((1, LANES), fixed),
                  pl.BlockSpec((1, SSM_INNER), fixed), pl.BlockSpec((1, SSM_INNER), fixed),
                  pl.BlockSpec((CHUNK, CHUNK), fixed), pl.BlockSpec((LANES, SSM_INNER), fixed)],
        out_specs=pl.BlockSpec((rb, SSM_INNER), row),
        out_shape=jax.ShapeDtypeStruct((t, SSM_INNER), BF16),
        scratch_shapes=[pltpu.VMEM((SSM_GROUPS, SSM_STATE, SSM_GROUP_WIDTH), F32),
                        pltpu.VMEM((rb + SUBLANES, SSM_CONV_CH), F32), pltpu.VMEM((rb, SSM_CONV_CH), F32)],
        compiler_params=_params("parallel", "arbitrary"),
        name="ssd_scan",
    )(z, xbc, dt, conv_w.astype(F32), conv_b.reshape(1, SSM_CONV_CH).astype(F32), pad_h(dt_bias), pad_h(a_log),
      jnp.repeat(d_skip.astype(F32), SSM_HEAD_DIM).reshape(1, SSM_INNER), norm_g.reshape(1, SSM_INNER).astype(F32),
      tri, expand)


def kernel(x, mix_norm, ffn_norm, ev_w_in, hg_lb_logits, hg_out_norm, at_q_norm, at_k_norm, ev_w_out, od_w_in,
           od_conv_w, od_conv_b, od_dt_bias, od_A_log, od_D, od_out_norm, od_w_out, moe_wg, moe_bg, moe_we,
           moe_be, moe_w1, moe_w3, moe_w2):
    batch, seq, d = x.shape
    assert d == D_MODEL and seq % HG_BLOCK == 0 and seq % KEY_TILE == 0 and (batch * seq) % ROUTER_TILE == 0
    depth = mix_norm.shape[0]
    x2d = x.reshape(batch * seq, d).astype(F32)
    lbs = jnp.cumsum(jax.nn.softmax(hg_lb_logits.astype(F32), axis=0), axis=0)
    for layer in range(depth):
        i = layer // 2
        if layer % 2 == 0:
            hproj, q, k, v, iq, ik, iw = _even_proj(x2d, mix_norm[layer], ev_w_in[i], at_q_norm[i], at_k_norm[i],
                                                    seq)
            o_h = _hgrn(hproj, lbs[i], hg_out_norm[i], batch, seq)
            o_a = _dsa(q, iq, iw, k, v, ik, batch, seq)
            w_h = ev_w_out[i][:HG_WIDTH].astype(BF16)
            w_a = jnp.pad(ev_w_out[i][HG_WIDTH:].reshape(AT_HEADS, AT_DIM, d),
                          ((0, 0), (0, LANES - AT_DIM), (0, 0))).reshape(AT_HEADS * LANES, d).astype(BF16)
            acts, weights = [o_h, o_a], [w_h, w_a]
        else:
            z, xbc, dt = _odd_proj(x2d, mix_norm[layer], od_w_in[i])
            y = _ssd(z, xbc, dt, od_conv_w[i], od_conv_b[i], od_dt_bias[i], od_A_log[i], od_D[i], od_out_norm[i],
                     batch, seq)
            acts, weights = [y], [od_w_out[i].astype(BF16)]
        x1, hn, route, counts = _out_router(x2d, acts, weights, ffn_norm[layer], moe_wg[layer], moe_bg[layer],
                                            moe_we[layer], moe_be[layer])
        x2d = _moe(x1, hn, route, counts, moe_w1, moe_w3, moe_w2, layer)
    return x2d.reshape(batch, seq, d).astype(x.dtype)
```

```python
import functools
import math

import jax
import jax.numpy as jnp
from jax import lax
from jax.experimental import pallas as pl
from jax.experimental.pallas import tpu as pltpu

D_MODEL = 1024
CHUNK = 64
Q_BLOCK = 128
ROPE_THETA = 10000.0
EPS = 1e-6

HG_HEADS = 4
HG_DIM = 128
HG_WIDTH = HG_HEADS * HG_DIM
AT_HEADS = 8
AT_KV_HEADS = 2
AT_GROUP = AT_HEADS // AT_KV_HEADS
AT_DIM = 64
AT_WIDTH = AT_HEADS * AT_DIM
KV_WIDTH = AT_KV_HEADS * AT_DIM
IDX_HEADS = 4
IDX_DIM = 64
TOPK_MAX = 256

SSM_INNER = 2 * D_MODEL
SSM_HEAD_DIM = 64
SSM_HEADS = SSM_INNER // SSM_HEAD_DIM
SSM_GROUPS = 4
SSM_STATE = 128
SSM_CONV = 4
SSM_CONV_CH = SSM_INNER + 2 * SSM_GROUPS * SSM_STATE
SSM_GROUP_WIDTH = SSM_INNER // SSM_GROUPS

MOE_GROUPS = 4
MOE_PER_GROUP = 8
MOE_EXPERTS = MOE_GROUPS * MOE_PER_GROUP
MOE_FF = 512

LANES = 128
SUBLANES = 8
VMEM_LIMIT = 56 * 1024 * 1024
ROW_TILE = 256
ROUTER_TILE = 512
KEY_TILE = 512
HG_BLOCK = 512
SSD_BLOCK = 256
EXPERT_TILE = 512
ROUTE_ROWS = 8 + MOE_EXPERTS
LOG2E = math.log2(math.e)
INT_MIN = -(2 ** 31)
NEG_BIG = -1e30

F32 = jnp.float32
BF16 = jnp.bfloat16
HIGHEST = lax.Precision.HIGHEST


def _params(*sem):
    return pltpu.CompilerParams(dimension_semantics=sem, vmem_limit_bytes=VMEM_LIMIT)


def _dot(a, b):
    return jnp.dot(a, b, preferred_element_type=F32)


def _dot_f32(a, b):
    return jnp.dot(a, b, preferred_element_type=F32, precision=HIGHEST)


def _bf16_pieces(x):
    hi = x.astype(BF16)
    rest = x - hi.astype(F32)
    mid = rest.astype(BF16)
    return hi, mid, (rest - mid.astype(F32)).astype(BF16)


def _select_rows(sel, x):
    hi, mid, lo = _bf16_pieces(x)
    return _dot(sel, hi) + _dot(sel, mid) + _dot(sel, lo)


def _select_cols(x, sel):
    hi, mid, lo = _bf16_pieces(x)
    return _dot(hi, sel) + _dot(mid, sel) + _dot(lo, sel)


def _dot_nt(a, b, precision=None):
    return lax.dot_general(a, b, (((1,), (1,)), ((), ())), preferred_element_type=F32, precision=precision)


def _dot_tn(a, b, precision=None):
    return lax.dot_general(a, b, (((0,), (0,)), ((), ())), preferred_element_type=F32, precision=precision)


def _sigmoid(x):
    return 1.0 / (1.0 + jnp.exp(-x))


def _silu(x):
    return x * _sigmoid(x)


def _rms_rows(x, gain):
    return x * lax.rsqrt(jnp.mean(x * x, axis=-1, keepdims=True) + EPS) * gain


def _rope_slab(x, cos, sin_lo, sin_hi):
    return x * cos + pltpu.roll(x, LANES - 32, 1) * sin_lo + pltpu.roll(x, 32, 1) * sin_hi


def _even_proj_kernel(x_ref, g_ref, wh_ref, wq_ref, wkv_ref, wiq_ref, wik_ref,
                      r1_ref, gm_ref, qn_ref, kn_ref,
                      h_ref, q_ref, k_ref, v_ref, iq_ref, ik_ref, iw_ref):
    xn = _rms_rows(x_ref[...], g_ref[...]).astype(BF16)
    h_ref[...] = _dot(xn, wh_ref[...])

    cos1, lo1, hi1 = r1_ref[0], r1_ref[1], r1_ref[2]
    gm = gm_ref[...]

    def qk_norm(v, gain):
        ms = _dot_f32(v * v, gm)
        return v * lax.rsqrt(ms + EPS) * gain

    q = _dot(xn, wq_ref[...])
    for h in range(AT_HEADS):
        sl = slice(h * LANES, (h + 1) * LANES)
        qh = _rope_slab(qk_norm(q[:, sl], qn_ref[...]), cos1, lo1, hi1)
        q_ref[:, sl] = (qh * (AT_DIM ** -0.5 * LOG2E)).astype(BF16)
    kv = _dot(xn, wkv_ref[...])
    for h in range(AT_KV_HEADS):
        sl = slice(h * LANES, (h + 1) * LANES)
        k_ref[:, sl] = _rope_slab(qk_norm(kv[:, sl], kn_ref[...]), cos1, lo1, hi1).astype(BF16)
    v = kv[:, AT_KV_HEADS * LANES:]
    lane = lax.broadcasted_iota(jnp.int32, v.shape, 1)
    v_ref[...] = jnp.where(lane % LANES == AT_DIM, 1.0, v).astype(BF16)
    iq = _dot(xn, wiq_ref[...])
    for h in range(IDX_HEADS):
        sl = slice(h * LANES, (h + 1) * LANES)
        iq_ref[:, sl] = (_rope_slab(iq[:, sl], cos1, lo1, hi1) * (IDX_DIM ** -0.5)).astype(BF16)
    ikw = _rope_slab(_dot(xn, wik_ref[...]), cos1, lo1, hi1)
    ik_ref[...] = ikw.astype(BF16)
    iw_ref[...] = ikw


def _pad_heads(w, heads):
    d = w.shape[0]
    w = w.reshape(d, heads, 64)
    return jnp.pad(w, ((0, 0), (0, 0), (0, 64))).reshape(d, heads * LANES)


def _rope_tables(seq):
    half = AT_DIM // 2
    inv = jnp.exp(-math.log(ROPE_THETA) * jnp.arange(half, dtype=F32) / half)
    ang = jnp.arange(seq, dtype=F32)[:, None] * inv[None, :]
    c, s = jnp.cos(ang), jnp.sin(ang)
    z, o = jnp.zeros_like(c), jnp.ones_like(c)
    return jnp.stack([jnp.concatenate([c, c, o, o], 1), jnp.concatenate([-s, z, z, z], 1),
                      jnp.concatenate([z, s, z, z], 1)])


def _even_proj(x2d, gain, w_in, q_norm, k_norm, seq):
    t = x2d.shape[0]
    tm = ROW_TILE
    o = [0, 4 * HG_WIDTH]
    for wdt in (AT_WIDTH, KV_WIDTH, KV_WIDTH, IDX_HEADS * IDX_DIM, IDX_DIM, IDX_HEADS):
        o.append(o[-1] + wdt)
    wh = w_in[:, o[0]:o[1]].astype(BF16)
    wq = _pad_heads(w_in[:, o[1]:o[2]], AT_HEADS).astype(BF16)
    wkv = jnp.concatenate([_pad_heads(w_in[:, o[2]:o[3]], AT_KV_HEADS),
                           _pad_heads(w_in[:, o[3]:o[4]], AT_KV_HEADS)], 1).astype(BF16)
    wiq = _pad_heads(w_in[:, o[4]:o[5]], IDX_HEADS).astype(BF16)
    wik = jnp.pad(w_in[:, o[5]:o[7]], ((0, 0), (0, LANES - IDX_DIM - IDX_HEADS))).astype(BF16)
    rope1 = _rope_tables(seq)
    lane = jnp.arange(LANES)
    gm = jnp.where((lane[:, None] < AT_DIM) & (lane[None, :] < AT_DIM), 1.0 / AT_DIM, 0.0).astype(F32)
    pad_gain = lambda g: jnp.pad(g, (0, LANES - AT_DIM), constant_values=1.0).reshape(1, LANES)
    nseq = seq // tm
    row = lambda i: (i, 0)
    fixed = lambda i: (0, 0)
    rope_map = lambda i: (0, i % nseq, 0)
    outs = pl.pallas_call(
        _even_proj_kernel,
        grid=(t // tm,),
        in_specs=[pl.BlockSpec((tm, D_MODEL), row), pl.BlockSpec((1, D_MODEL), fixed),
                  pl.BlockSpec(wh.shape, fixed), pl.BlockSpec(wq.shape, fixed), pl.BlockSpec(wkv.shape, fixed),
                  pl.BlockSpec(wiq.shape, fixed), pl.BlockSpec(wik.shape, fixed),
                  pl.BlockSpec((3, tm, LANES), rope_map),
                  pl.BlockSpec((LANES, LANES), fixed), pl.BlockSpec((1, LANES), fixed),
                  pl.BlockSpec((1, LANES), fixed)],
        out_specs=[pl.BlockSpec((tm, 4 * HG_WIDTH), row), pl.BlockSpec((tm, AT_HEADS * LANES), row),
                   pl.BlockSpec((tm, AT_KV_HEADS * LANES), row), pl.BlockSpec((tm, AT_KV_HEADS * LANES), row),
                   pl.BlockSpec((tm, IDX_HEADS * LANES), row),
                   pl.BlockSpec((tm, LANES), row), pl.BlockSpec((tm, LANES), row)],
        out_shape=[jax.ShapeDtypeStruct((t, 4 * HG_WIDTH), F32), jax.ShapeDtypeStruct((t, AT_HEADS * LANES), BF16),
                   jax.ShapeDtypeStruct((t, AT_KV_HEADS * LANES), BF16),
                   jax.ShapeDtypeStruct((t, AT_KV_HEADS * LANES), BF16),
                   jax.ShapeDtypeStruct((t, IDX_HEADS * LANES), BF16),
                   jax.ShapeDtypeStruct((t, LANES), BF16), jax.ShapeDtypeStruct((t, LANES), F32)],
        compiler_params=_params("parallel"),
        name="even_proj",
    )(x2d, gain.reshape(1, D_MODEL), wh, wq, wkv, wiq, wik, rope1, gm, pad_gain(q_norm), pad_gain(k_norm))
    return outs


HG_SUB = 16
HG_SAFE_SPAN = 60.0


def _hgrn_kernel(h_ref, lb_ref, gn_ref, tri_ref, o_ref, st_ref, od_ref):
    @pl.when(pl.program_id(1) == 0)
    def _():
        st_ref[...] = jnp.zeros_like(st_ref)

    tri = tri_ref[...]
    s_idx = lax.broadcasted_iota(jnp.int32, (HG_SUB, 1), 0)

    t_idx = lax.broadcasted_iota(jnp.int32, (CHUNK, CHUNK), 0)
    c_idx = lax.broadcasted_iota(jnp.int32, (CHUNK, CHUNK), 1)

    def col(stream, h):
        return slice((stream * HG_HEADS + h) * HG_DIM, (stream * HG_HEADS + h + 1) * HG_DIM)

    def chunk(c, carry):
        r0 = pl.multiple_of(c * CHUNK, CHUNK)
        rows = pl.ds(r0, CHUNK)
        gates = []
        for h in range(HG_HEADS):
            lb = lb_ref[:, h * HG_DIM:(h + 1) * HG_DIM]
            hf = h_ref[rows, col(1, h)]
            sg = _sigmoid(hf)
            f = lb + (1.0 - lb) * sg
            kk = (1.0 - lb) * (1.0 - sg)
            gates.append((kk, _select_rows(tri, jnp.log(f))))
        floor = jnp.min(jnp.concatenate([b[CHUNK - 1:CHUNK] for _, b in gates], axis=1))

        @pl.when(floor >= -HG_SAFE_SPAN)
        def _():
            for h in range(HG_HEADS):
                head_chunk(rows, h, *gates[h], intra_split)

        @pl.when(floor < -HG_SAFE_SPAN)
        def _():
            for h in range(HG_HEADS):
                head_chunk(rows, h, *gates[h], intra_exact)

        return carry

    def intra_split(h, q, kk, v, b):
        del h
        a = _dot_nt((q * jnp.exp(b)).astype(BF16), (kk * jnp.exp(-b)).astype(BF16))
        return _dot(jnp.where(c_idx <= t_idx, a, 0.0).astype(BF16), v.astype(BF16))

    def head_chunk(rows, h, kk, b, intra):
        sl = slice(h * HG_DIM, (h + 1) * HG_DIM)
        q = h_ref[rows, col(0, h)]
        v = h_ref[rows, col(2, h)]
        st = st_ref[h]
        o = _dot_nt((q * jnp.exp(b)).astype(BF16), st.astype(BF16)) + intra(h, q, kk, v, b)
        b_last = b[CHUNK - 1:CHUNK, :]
        st_ref[h] = st * jnp.exp(b_last) + _dot_tn(v.astype(BF16), (kk * jnp.exp(b_last - b)).astype(BF16))
        o = _rms_rows(o, gn_ref[:, sl]) * _silu(h_ref[rows, col(3, h)])
        o_ref[rows, sl] = o.astype(o_ref.dtype)

    def intra_exact(h, q, kk, v, b):
        od = od_ref.at[h]
        off = []
        for blk in range(CHUNK // HG_SUB):
            lo, hi = blk * HG_SUB, (blk + 1) * HG_SUB
            bs, ks, vs = b[lo:hi], kk[lo:hi], v[lo:hi]
            for tt in range(HG_SUB):
                t = lo + tt
                e = jnp.exp(jnp.minimum(b[t:t + 1] - bs, 0.0))
                a = jnp.sum(e * (q[t:t + 1] * ks), axis=-1, keepdims=True)
                a = jnp.where(s_idx <= tt, a, 0.0)
                od[t:t + 1, :] = jnp.sum(a * vs, axis=0, keepdims=True)
            if blk == 0:
                off.append(jnp.zeros((HG_SUB, HG_DIM), F32))
            else:
                ref = b[lo - 1:lo]
                qd = (q[lo:hi] * jnp.exp(b[lo:hi] - ref)).astype(BF16)
                kd = (kk[:lo] * jnp.exp(ref - b[:lo])).astype(BF16)
                off.append(_dot(_dot_nt(qd, kd).astype(BF16), v[:lo].astype(BF16)))
        return od[...] + jnp.concatenate(off, axis=0)

    lax.fori_loop(0, HG_BLOCK // CHUNK, chunk, 0)


def _hgrn(hproj, lb, out_norm, batch, seq):
    t = hproj.shape[0]
    nblk = seq // HG_BLOCK
    tri = jnp.tril(jnp.ones((CHUNK, CHUNK), BF16))
    row = lambda b, s: (b * nblk + s, 0)
    fixed = lambda b, s: (0, 0)
    return pl.pallas_call(
        _hgrn_kernel,
        grid=(batch, nblk),
        in_specs=[pl.BlockSpec((HG_BLOCK, 4 * HG_WIDTH), row), pl.BlockSpec((1, HG_WIDTH), fixed),
                  pl.BlockSpec((1, HG_WIDTH), fixed), pl.BlockSpec((CHUNK, CHUNK), fixed)],
        out_specs=pl.BlockSpec((HG_BLOCK, HG_WIDTH), row),
        out_shape=jax.ShapeDtypeStruct((t, HG_WIDTH), BF16),
        scratch_shapes=[pltpu.VMEM((HG_HEADS, HG_DIM, HG_DIM), F32), pltpu.VMEM((HG_HEADS, CHUNK, HG_DIM), F32)],
        compiler_params=_params("parallel", "arbitrary"),
        name="hgrn2_scan",
    )(hproj, lb.reshape(1, HG_WIDTH), out_norm.reshape(1, HG_WIDTH), tri)


COUNT_LANES = 4
SEARCH_UNTESTED_BITS = 20


def _dsa_kernel(q_ref, iq_ref, iw_ref, k_ref, vt_ref, ik_ref, tril_ref, o_ref, keys_ref, *, topk):
    j = pl.program_id(1)
    kb = KEY_TILE
    nkt = (j * Q_BLOCK + Q_BLOCK + kb - 1) // kb
    qrow = lax.broadcasted_iota(jnp.int32, (1, Q_BLOCK), 1)
    nadm = ((j * Q_BLOCK + qrow) // CHUNK + 1) * CHUNK
    key_row = lax.broadcasted_iota(jnp.int32, (kb, Q_BLOCK), 0)
    iwt = jnp.transpose(iw_ref[...])
    iws = [iwt[IDX_DIM + h:IDX_DIM + h + 1, :] for h in range(IDX_HEADS)]

    iq_pairs = [jnp.concatenate([iq_ref[:, (2 * p + u) * LANES:(2 * p + u + 1) * LANES] for u in range(2)], axis=0)
                for p in range(IDX_HEADS // 2)]

    def score_tile(kt, carry):
        ik = ik_ref[kt]
        score = jnp.zeros((kb, Q_BLOCK), F32)
        for p in range(IDX_HEADS // 2):
            sc = _dot_nt(ik, iq_pairs[p])
            for u in range(2):
                score = score + iws[2 * p + u] * jnp.maximum(sc[:, u * Q_BLOCK:(u + 1) * Q_BLOCK], 0.0)
        bits = pltpu.bitcast(score, jnp.int32)
        key = jnp.where(bits < 0, bits ^ 0x7FFFFFFF, bits)
        key = jnp.where(bits == INT_MIN, 0, key)
        keys_ref[kt] = jnp.where(kt * kb + key_row < nadm, key, INT_MIN)
        return carry

    lax.fori_loop(0, nkt, score_tile, 0)

    @pl.when(nkt % 2 == 1)
    def _():
        keys_ref[nkt] = jnp.full((kb, Q_BLOCK), INT_MIN, jnp.int32)

    def fold(ind):
        return jnp.sum(ind.reshape(COUNT_LANES, kb // (COUNT_LANES * SUBLANES), SUBLANES, Q_BLOCK), axis=1)

    def count(indicator):
        def body(p, acc):
            return acc + fold(indicator(keys_ref[2 * p])) + fold(indicator(keys_ref[2 * p + 1]))
        acc = lax.fori_loop(0, (nkt + 1) // 2, body, jnp.zeros((COUNT_LANES, SUBLANES, Q_BLOCK), F32))
        return jnp.sum(jnp.sum(acc, axis=0), axis=0, keepdims=True)

    zero_ge = count(lambda ks: jnp.where(ks >= 0, 1.0, 0.0))
    zero_gt = count(lambda ks: jnp.where(ks > 0, 1.0, 0.0))
    at_zero = jnp.where(zero_gt < topk, jnp.where(zero_ge >= topk, 1.0, 0.0), 0.0)
    settled0 = jnp.where(nadm < topk, 1.0, at_zero)
    found0 = jnp.where(nadm < topk, INT_MIN, 0)

    def search_bit(i, st):
        tu, found, settled = st
        cand = tu | jnp.left_shift(jnp.int32(1), 31 - i)
        cand_s = cand ^ INT_MIN
        cnt = count(lambda ks: jnp.where(ks >= cand_s, 1.0, 0.0))
        hit = jnp.where(settled > 0.5, 0.0, jnp.where(cnt == topk, 1.0, 0.0))
        found = jnp.where(hit > 0.5, cand_s, found)
        settled = jnp.maximum(settled, hit)
        tu = jnp.where(cnt >= topk, cand, tu)
        return tu, found, settled

    tu0 = jnp.where(zero_ge >= topk, INT_MIN, 0)
    st = lax.fori_loop(1, SEARCH_UNTESTED_BITS, search_bit, (tu0, found0, settled0))

    def search_cond(c):
        i, _, open_rows = c
        return jnp.logical_and(i < 32, open_rows > 0.5)

    def search_step(c):
        i, st, _ = c
        st = search_bit(i, st)
        return i + 1, st, jnp.max(1.0 - st[2])

    _, (tu, found, settled), _ = lax.while_loop(
        search_cond, search_step, (jnp.int32(SEARCH_UNTESTED_BITS), st, jnp.max(1.0 - st[2])))
    thr = jnp.where(settled > 0.5, found, tu ^ INT_MIN)
    need = topk - count(lambda ks: jnp.where(ks > thr, 1.0, 0.0))

    rows4 = AT_GROUP * Q_BLOCK
    qs = [jnp.concatenate([q_ref[:, (g * AT_GROUP + hh) * LANES:(g * AT_GROUP + hh + 1) * LANES]
                           for hh in range(AT_GROUP)], axis=0) for g in range(AT_KV_HEADS)]

    def masked_scores(kt, seen):
        ks = keys_ref[kt]
        tied = jnp.where(ks == thr, 1.0, 0.0).astype(BF16)
        slabs = []
        for sb in range(kb // LANES):
            seen = _dot(tril_ref[...], tied[sb * LANES:(sb + 1) * LANES]) + seen[-1:]
            slabs.append(seen)
        rank = jnp.concatenate(slabs, axis=0)
        seen = seen[-1:]
        tie = jnp.where(ks == thr, jnp.where(rank <= need, 0.0, NEG_BIG), NEG_BIG)
        bias = jnp.where(kt * kb + key_row < nadm, jnp.where(ks > thr, 0.0, tie), NEG_BIG)
        bias4 = jnp.concatenate([bias] * AT_GROUP, axis=1)
        k_tile = k_ref[kt]
        return seen, tuple(_dot_nt(k_tile[:, g * LANES:(g + 1) * LANES], qs[g]) + bias4
                           for g in range(AT_KV_HEADS))

    def att_tile(kt, carry):
        seen, state = carry
        seen, scores = masked_scores(kt, seen)
        vt_tile = vt_ref[kt]
        out = []
        for g in range(AT_KV_HEADS):
            m, acc = state[g]
            s = scores[g]
            m_new = jnp.maximum(m, jnp.max(s, axis=0, keepdims=True))
            alpha = jnp.exp2(m - m_new)
            p = jnp.exp2(s - m_new).astype(BF16)
            acc = alpha * acc + _dot(vt_tile[g * LANES:(g + 1) * LANES, :], p)
            out.append((m_new, acc))
        return seen, tuple(out)

    init = tuple((jnp.full((1, rows4), 0.1 * NEG_BIG, F32), jnp.zeros((LANES, rows4), F32))
                 for _ in range(AT_KV_HEADS))
    _, res = lax.fori_loop(0, (nkt + 1) // 2, lambda p, c: att_tile(2 * p + 1, att_tile(2 * p, c)),
                           (jnp.zeros((1, Q_BLOCK), F32), init))
    for g in range(AT_KV_HEADS):
        _, acc = res[g]
        og = acc / acc[AT_DIM:AT_DIM + 1, :]
        for hh in range(AT_GROUP):
            h = g * AT_GROUP + hh
            o_ref[:, h * LANES:(h + 1) * LANES] = jnp.transpose(og[:, hh * Q_BLOCK:(hh + 1) * Q_BLOCK]).astype(o_ref.dtype)


def _dsa(q, iq, iw, k, v, ik, batch, seq):
    t = q.shape[0]
    nb = seq // Q_BLOCK
    nkt = seq // KEY_TILE
    topk = min(TOPK_MAX, seq // 4)
    width = AT_KV_HEADS * LANES
    k3 = k.reshape(batch * nkt, KEY_TILE, width)
    vt3 = jnp.swapaxes(v.reshape(batch * nkt, KEY_TILE, width), 1, 2)
    ik3 = ik.reshape(batch * nkt, KEY_TILE, LANES)
    ki = jnp.arange(LANES)
    tril = (ki[None, :] <= ki[:, None]).astype(BF16)
    qrow = lambda b, j: (b * nb + j, 0)
    seq_blk = lambda b, j: (b, 0, 0)
    return pl.pallas_call(
        functools.partial(_dsa_kernel, topk=topk),
        grid=(batch, nb),
        in_specs=[pl.BlockSpec((Q_BLOCK, AT_HEADS * LANES), qrow), pl.BlockSpec((Q_BLOCK, IDX_HEADS * LANES), qrow),
                  pl.BlockSpec((Q_BLOCK, LANES), qrow),
                  pl.BlockSpec((nkt, KEY_TILE, width), seq_blk), pl.BlockSpec((nkt, width, KEY_TILE), seq_blk),
                  pl.BlockSpec((nkt, KEY_TILE, LANES), seq_blk),
                  pl.BlockSpec((LANES, LANES), lambda b, j: (0, 0))],
        out_specs=pl.BlockSpec((Q_BLOCK, AT_HEADS * LANES), qrow),
        out_shape=jax.ShapeDtypeStruct((t, AT_HEADS * LANES), BF16),
        scratch_shapes=[pltpu.VMEM((nkt + 1, KEY_TILE, Q_BLOCK), jnp.int32)],
        compiler_params=_params("parallel", "arbitrary"),
        name="dsa_attention",
    )(q, iq, iw, k3, vt3, ik3, tril)


def _out_router_kernel(*refs, n_in):
    x_ref = refs[0]
    a_refs = refs[1:1 + n_in]
    w_refs = refs[1 + n_in:1 + 2 * n_in]
    g_ref, wr_ref, br_ref, su_ref = refs[1 + 2 * n_in:5 + 2 * n_in]
    x1_ref, hn_ref, route_ref, cnt_ref = refs[5 + 2 * n_in:9 + 2 * n_in]
    carry_ref = refs[9 + 2 * n_in]

    @pl.when(pl.program_id(0) == 0)
    def _():
        carry_ref[...] = jnp.zeros_like(carry_ref)

    x1 = x_ref[...]
    for a_ref, w_ref in zip(a_refs, w_refs):
        x1 = x1 + _dot(a_ref[...], w_ref[...])
    x1_ref[...] = x1
    hn = _rms_rows(x1, g_ref[...])
    hn_ref[...] = hn

    lt = _dot_nt(wr_ref[...], hn, precision=HIGHEST) + br_ref[...]
    tm = lt.shape[1]
    r4 = lax.broadcasted_iota(jnp.int32, (MOE_GROUPS, tm), 0).astype(F32)
    gl = lt[0:MOE_GROUPS]
    gmax = jnp.max(gl, axis=0, keepdims=True)
    gidx = jnp.min(jnp.where(gl == gmax, r4, float(MOE_GROUPS)), axis=0, keepdims=True)
    gval = 1.0 / jnp.sum(jnp.exp(gl - gmax), axis=0, keepdims=True)
    el = jnp.zeros((MOE_PER_GROUP, tm), F32)
    for g in range(MOE_GROUPS):
        el = el + jnp.where(gidx == float(g), lt[8 + g * MOE_PER_GROUP:8 + (g + 1) * MOE_PER_GROUP], 0.0)
    r8 = lax.broadcasted_iota(jnp.int32, (MOE_PER_GROUP, tm), 0).astype(F32)
    m1 = jnp.max(el, axis=0, keepdims=True)
    i1 = jnp.min(jnp.where(el == m1, r8, float(MOE_PER_GROUP)), axis=0, keepdims=True)
    el2 = jnp.where(r8 == i1, -jnp.inf, el)
    m2 = jnp.max(el2, axis=0, keepdims=True)
    i2 = jnp.min(jnp.where(el2 == m2, r8, float(MOE_PER_GROUP)), axis=0, keepdims=True)
    e21 = jnp.exp(m2 - m1)
    wa = gval / (1.0 + e21)
    wb = gval * e21 / (1.0 + e21)
    fa = gidx * MOE_PER_GROUP + i1
    fb = gidx * MOE_PER_GROUP + i2
    re = lax.broadcasted_iota(jnp.int32, (MOE_EXPERTS, tm), 0).astype(F32)
    hit_a = re == fa
    hit_b = re == fb
    member = jnp.where(hit_a, 1.0, jnp.where(hit_b, 1.0, 0.0))
    before = _dot(member.astype(BF16), su_ref[...]) + carry_ref[...]
    rank_a = jnp.sum(jnp.where(hit_a, before, 0.0), axis=0, keepdims=True)
    rank_b = jnp.sum(jnp.where(hit_b, before, 0.0), axis=0, keepdims=True)
    carry_ref[...] = carry_ref[...] + jnp.sum(member, axis=1, keepdims=True)
    cnt_ref[...] = jnp.broadcast_to(carry_ref[...], cnt_ref.shape)
    zero = jnp.zeros_like(wa)
    route_ref[...] = jnp.concatenate([fa, fb, wa, wb, rank_a, rank_b, zero, zero], axis=0)


def _out_router(x2d, acts, weights, ffn_gain, wg, bg, we, be):
    t = x2d.shape[0]
    tm = ROUTER_TILE
    n_in = len(acts)
    wr = jnp.concatenate([wg.T, jnp.zeros((8 - MOE_GROUPS, D_MODEL), F32), we.T], axis=0)
    br = jnp.concatenate([bg, jnp.zeros((8 - MOE_GROUPS,), F32), be]).reshape(ROUTE_ROWS, 1).astype(F32)
    ti = jnp.arange(tm)
    su = (ti[:, None] < ti[None, :]).astype(BF16)
    row = lambda i: (i, 0)
    fixed = lambda i: (0, 0)
    in_specs = [pl.BlockSpec((tm, D_MODEL), row)]
    in_specs += [pl.BlockSpec((tm, a.shape[1]), row) for a in acts]
    in_specs += [pl.BlockSpec(w.shape, fixed) for w in weights]
    in_specs += [pl.BlockSpec((1, D_MODEL), fixed), pl.BlockSpec(wr.shape, fixed), pl.BlockSpec(br.shape, fixed),
                 pl.BlockSpec(su.shape, fixed)]
    x1, hn, route, cnt = pl.pallas_call(
        functools.partial(_out_router_kernel, n_in=n_in),
        grid=(t // tm,),
        in_specs=in_specs,
        out_specs=[pl.BlockSpec((tm, D_MODEL), row), pl.BlockSpec((tm, D_MODEL), row),
                   pl.BlockSpec((8, tm), lambda i: (0, i)), pl.BlockSpec((MOE_EXPERTS, LANES), fixed)],
        out_shape=[jax.ShapeDtypeStruct((t, D_MODEL), F32), jax.ShapeDtypeStruct((t, D_MODEL), F32),
                   jax.ShapeDtypeStruct((8, t), F32), jax.ShapeDtypeStruct((MOE_EXPERTS, LANES), F32)],
        scratch_shapes=[pltpu.VMEM((MOE_EXPERTS, 1), F32)],
        compiler_params=_params("arbitrary"),
        name="out_proj_router",
    )(x2d, *acts, *weights, ffn_gain.reshape(1, D_MODEL), wr.astype(F32), br, su)
    return x1, hn, route, cnt[:, 0]


def _row_copy(src_ref, src_row, dst_ref, dst_row, sem):
    return pltpu.make_async_copy(src_ref.at[pl.ds(src_row, 1), :], dst_ref.at[pl.ds(dst_row, 1), :], sem)


ROWS_PER_ISSUE = 8


def _for_rows(n, fn):
    def body(i, c):
        for u in range(ROWS_PER_ISSUE):
            fn(i * ROWS_PER_ISSUE + u)
        return c

    lax.fori_loop(0, n // ROWS_PER_ISSUE, body, 0)


def _dispatch_kernel(sa_ref, sb_ref, hn_ref, xs_in_ref, xs_ref, sem):
    del xs_in_ref
    tm = hn_ref.shape[0]

    def copies(r):
        return (_row_copy(hn_ref, r, xs_ref, sa_ref[0, 0, r], sem), _row_copy(hn_ref, r, xs_ref, sb_ref[0, 0, r], sem))

    def start(r):
        for cp in copies(r):
            cp.start()

    def wait(r):
        for cp in copies(r):
            cp.wait()

    _for_rows(tm, start)
    _for_rows(tm, wait)


def _dispatch(hn, slot_a, slot_b, n_slots):
    t = hn.shape[0]
    tm = ROW_TILE
    smem = lambda: pl.BlockSpec((1, 1, tm), lambda i: (i, 0, 0), memory_space=pltpu.SMEM)
    xs0 = jnp.zeros((n_slots, D_MODEL), hn.dtype)
    return pl.pallas_call(
        _dispatch_kernel,
        grid=(t // tm,),
        in_specs=[smem(), smem(), pl.BlockSpec((tm, D_MODEL), lambda i: (i, 0)), pl.BlockSpec(memory_space=pl.ANY)],
        out_specs=pl.BlockSpec(memory_space=pl.ANY),
        out_shape=jax.ShapeDtypeStruct((n_slots, D_MODEL), hn.dtype),
        scratch_shapes=[pltpu.SemaphoreType.DMA(())],
        input_output_aliases={3: 0},
        compiler_params=_params("arbitrary"),
        name="moe_dispatch",
    )(slot_a.reshape(t // tm, 1, tm), slot_b.reshape(t // tm, 1, tm), hn, xs0)


def _ffn_kernel(te_ref, nv_ref, xs_ref, w1_ref, w3_ref, w2_ref, ys_ref, b1_ref, b3_ref, b2_ref):
    i = pl.program_id(0)

    @pl.when(jnp.logical_or(i == 0, te_ref[i] != te_ref[jnp.maximum(i - 1, 0)]))
    def _():
        b1_ref[...] = w1_ref[...].astype(BF16)
        b3_ref[...] = w3_ref[...].astype(BF16)
        b2_ref[...] = w2_ref[...].astype(BF16)

    @pl.when(i < nv_ref[0])
    def _():
        x = xs_ref[...].astype(BF16)
        a = _dot(x, b1_ref[...])
        b = _dot(x, b3_ref[...])
        ys_ref[...] = _dot((_silu(a) * b).astype(BF16), b2_ref[...])

    @pl.when(pl.program_id(0) >= nv_ref[0])
    def _():
        ys_ref[...] = jnp.zeros_like(ys_ref)


def _expert_ffn(xs, tile_expert, n_valid, w1, w3, w2):
    n_slots = xs.shape[0]
    n_tiles = n_slots // EXPERT_TILE
    wmap = lambda i, te, nv: (te[i], 0, 0)
    grid_spec = pltpu.PrefetchScalarGridSpec(
        num_scalar_prefetch=2,
        grid=(n_tiles,),
        in_specs=[pl.BlockSpec((EXPERT_TILE, D_MODEL), lambda i, te, nv: (i, 0)),
                  pl.BlockSpec((None, D_MODEL, MOE_FF), wmap), pl.BlockSpec((None, D_MODEL, MOE_FF), wmap),
                  pl.BlockSpec((None, MOE_FF, D_MODEL), wmap)],
        out_specs=pl.BlockSpec((EXPERT_TILE, D_MODEL), lambda i, te, nv: (i, 0)),
        scratch_shapes=[pltpu.VMEM((D_MODEL, MOE_FF), BF16), pltpu.VMEM((D_MODEL, MOE_FF), BF16),
                        pltpu.VMEM((MOE_FF, D_MODEL), BF16)],
    )
    return pl.pallas_call(
        _ffn_kernel,
        grid_spec=grid_spec,
        out_shape=jax.ShapeDtypeStruct((n_slots, D_MODEL), F32),
        compiler_params=_params("arbitrary"),
        name="moe_expert_ffn",
    )(tile_expert, n_valid, xs, w1, w3, w2)


def _combine_kernel(sa_ref, sb_ref, x_ref, wt_ref, ys_ref, o_ref, ya_ref, yb_ref, sem):
    tm = x_ref.shape[0]

    def copies(r):
        return (_row_copy(ys_ref, sa_ref[0, 0, r], ya_ref, r, sem), _row_copy(ys_ref, sb_ref[0, 0, r], yb_ref, r, sem))

    def start(r):
        for cp in copies(r):
            cp.start()

    def wait(r):
        for cp in copies(r):
            cp.wait()

    _for_rows(tm, start)
    _for_rows(tm, wait)
    wt = wt_ref[...]
    o_ref[...] = x_ref[...] + wt[:, 2:3] * ya_ref[...] + wt[:, 3:4] * yb_ref[...]


def _combine(x1, ys, slot_a, slot_b, wts):
    t = x1.shape[0]
    tm = ROW_TILE
    smem = lambda: pl.BlockSpec((1, 1, tm), lambda i: (i, 0, 0), memory_space=pltpu.SMEM)
    row = lambda i: (i, 0)
    return pl.pallas_call(
        _combine_kernel,
        grid=(t // tm,),
        in_specs=[smem(), smem(), pl.BlockSpec((tm, D_MODEL), row), pl.BlockSpec((tm, 8), row),
                  pl.BlockSpec(memory_space=pl.ANY)],
        out_specs=pl.BlockSpec((tm, D_MODEL), row),
        out_shape=jax.ShapeDtypeStruct((t, D_MODEL), F32),
        scratch_shapes=[pltpu.VMEM((tm, D_MODEL), F32), pltpu.VMEM((tm, D_MODEL), F32),
                        pltpu.SemaphoreType.DMA(())],
        compiler_params=_params("arbitrary"),
        name="moe_combine",
    )(slot_a.reshape(t // tm, 1, tm), slot_b.reshape(t // tm, 1, tm), x1, wts, ys)


def _moe(x1, hn, route, counts, w1, w3, w2, layer):
    t = x1.shape[0]
    n_tiles = (2 * t) // EXPERT_TILE + MOE_EXPERTS
    n_slots = n_tiles * EXPERT_TILE
    counts = counts.astype(jnp.int32)
    tiles_per = (counts + EXPERT_TILE - 1) // EXPERT_TILE
    ei = jnp.arange(MOE_EXPERTS, dtype=jnp.int32)
    tile_end = jnp.sum(jnp.where(ei[None, :] <= ei[:, None], tiles_per[None, :], 0), axis=1)
    seg_start = (tile_end - tiles_per) * EXPERT_TILE
    n_valid = tile_end[-1:]
    tile_idx = jnp.arange(n_tiles, dtype=jnp.int32)
    tile_expert = jnp.minimum(jnp.sum((tile_idx[:, None] >= tile_end[None, :]).astype(jnp.int32), axis=1),
                              MOE_EXPERTS - 1)
    ri = route.astype(jnp.int32)
    start_of = lambda e: jnp.sum(jnp.where(e[:, None] == ei[None, :], seg_start[None, :], 0), axis=1)
    slot_a = start_of(ri[0]) + ri[4]
    slot_b = start_of(ri[1]) + ri[5]
    xs = _dispatch(hn, slot_a, slot_b, n_slots)
    ys = _expert_ffn(xs, tile_expert + layer * MOE_EXPERTS, n_valid, w1.reshape(-1, D_MODEL, MOE_FF),
                     w3.reshape(-1, D_MODEL, MOE_FF), w2.reshape(-1, MOE_FF, D_MODEL))
    return _combine(x1, ys, slot_a, slot_b, jnp.transpose(route))


def _odd_proj_kernel(x_ref, g_ref, wz_ref, wx_ref, wd_ref, z_ref, xbc_ref, dt_ref):
    xn = _rms_rows(x_ref[...], g_ref[...]).astype(BF16)
    z_ref[...] = _dot(xn, wz_ref[...])
    xbc_ref[...] = _dot(xn, wx_ref[...])
    dt_ref[...] = _dot(xn, wd_ref[...])


def _odd_proj(x2d, gain, w_in):
    t = x2d.shape[0]
    tm = ROW_TILE
    wz = w_in[:, :SSM_INNER].astype(BF16)
    wx = w_in[:, SSM_INNER:SSM_INNER + SSM_CONV_CH].astype(BF16)
    wd = jnp.pad(w_in[:, SSM_INNER + SSM_CONV_CH:], ((0, 0), (0, LANES - SSM_HEADS))).astype(BF16)
    row = lambda i: (i, 0)
    fixed = lambda i: (0, 0)
    return pl.pallas_call(
        _odd_proj_kernel,
        grid=(t // tm,),
        in_specs=[pl.BlockSpec((tm, D_MODEL), row), pl.BlockSpec((1, D_MODEL), fixed),
                  pl.BlockSpec(wz.shape, fixed), pl.BlockSpec(wx.shape, fixed), pl.BlockSpec(wd.shape, fixed)],
        out_specs=[pl.BlockSpec((tm, SSM_INNER), row), pl.BlockSpec((tm, SSM_CONV_CH), row),
                   pl.BlockSpec((tm, LANES), row)],
        out_shape=[jax.ShapeDtypeStruct((t, SSM_INNER), F32), jax.ShapeDtypeStruct((t, SSM_CONV_CH), F32),
                   jax.ShapeDtypeStruct((t, LANES), F32)],
        compiler_params=_params("parallel"),
        name="odd_proj",
    )(x2d, gain.reshape(1, D_MODEL), wz, wx, wd)


def _ssd_kernel(z_ref, xbc_ref, dt_ref, cw_ref, cb_ref, dtb_ref, a_ref, dskip_ref, ng_ref, tri_ref, exp_ref,
                y_ref, st_ref, ext_ref, xc_ref):
    rows_blk = xbc_ref.shape[0]

    @pl.when(pl.program_id(1) == 0)
    def _():
        st_ref[...] = jnp.zeros_like(st_ref)
        ext_ref[0:SUBLANES, :] = jnp.zeros((SUBLANES, SSM_CONV_CH), F32)

    cur = xbc_ref[...]
    ext_ref[SUBLANES:, :] = cur
    acc = cb_ref[...] + cw_ref[SSM_CONV - 1:SSM_CONV, :] * cur
    for k in range(1, SSM_CONV):
        acc = acc + cw_ref[SSM_CONV - 1 - k:SSM_CONV - k, :] * ext_ref[SUBLANES - k:SUBLANES - k + rows_blk, :]
    ext_ref[0:SUBLANES, :] = cur[rows_blk - SUBLANES:]
    xc_ref[...] = _silu(acc)

    tri = tri_ref[...]
    expand = exp_ref[...]
    a_neg = -jnp.exp(a_ref[...])
    lane = lax.broadcasted_iota(jnp.int32, (CHUNK, LANES), 1)
    trow = lax.broadcasted_iota(jnp.int32, (CHUNK, LANES), 0)
    left = lane < SSM_HEAD_DIM
    diag = trow == jnp.where(left, lane, lane - SSM_HEAD_DIM)
    causal = trow >= jnp.where(left, lane, lane - SSM_HEAD_DIM)
    heads_per_group = SSM_HEADS // SSM_GROUPS
    b_off = SSM_INNER
    c_off = SSM_INNER + SSM_GROUPS * SSM_STATE

    def chunk(c, carry):
        r0 = pl.multiple_of(c * CHUNK, CHUNK)
        rows = pl.ds(r0, CHUNK)
        dt = dt_ref[rows, :] + dtb_ref[...]
        dt = jnp.maximum(dt, 0.0) + jnp.log(1.0 + jnp.exp(-jnp.abs(dt)))
        cum = _select_cols(_select_rows(tri, dt * a_neg), expand)
        dtx = _dot(dt.astype(BF16), expand)
        xs = xc_ref[rows, 0:SSM_INNER]
        xdt = xs * dtx
        cum_last = cum[CHUNK - 1:CHUNK, :]
        decay_in = jnp.exp(cum)
        decay_out = jnp.exp(cum_last - cum)
        xw = (xdt * decay_out).astype(BF16)
        y_parts = []
        for g in range(SSM_GROUPS):
            gl = slice(g * SSM_GROUP_WIDTH, (g + 1) * SSM_GROUP_WIDTH)
            bm = xc_ref[rows, b_off + g * SSM_STATE:b_off + (g + 1) * SSM_STATE].astype(BF16)
            cm = xc_ref[rows, c_off + g * SSM_STATE:c_off + (g + 1) * SSM_STATE].astype(BF16)
            st = st_ref[g]
            y_g = _dot(cm, st.astype(BF16)) * decay_in[:, gl]
            st_ref[g] = st * jnp.exp(cum_last[:, gl]) + _dot_tn(bm, xw[:, gl])
            cb2 = _dot_nt(cm, jnp.concatenate([bm, bm], axis=0))
            pair_out = []
            for pr in range(heads_per_group // 2):
                sl = slice(g * SSM_GROUP_WIDTH + pr * LANES, g * SSM_GROUP_WIDTH + (pr + 1) * LANES)
                cum_p = cum[:, sl]
                cum_row = jnp.sum(jnp.where(diag, cum_p, 0.0), axis=0, keepdims=True)
                lmat = jnp.where(causal, jnp.exp(jnp.minimum(cum_p - cum_row, 0.0)), 0.0)
                xp = xdt[:, sl]
                x2 = jnp.concatenate([jnp.where(left, xp, 0.0), jnp.where(left, 0.0, xp)], axis=0).astype(BF16)
                pair_out.append(_dot((cb2 * lmat).astype(BF16), x2))
            y_parts.append(y_g + jnp.concatenate(pair_out, axis=1))
        y = jnp.concatenate(y_parts, axis=1) + dskip_ref[...] * xs
        y = y * _silu(z_ref[rows, :])
        outs = []
        for g in range(SSM_GROUPS):
            gl = slice(g * SSM_GROUP_WIDTH, (g + 1) * SSM_GROUP_WIDTH)
            outs.append(_rms_rows(y[:, gl], ng_ref[:, gl]))
        y_ref[rows, :] = jnp.concatenate(outs, axis=1).astype(y_ref.dtype)
        return carry

    lax.fori_loop(0, rows_blk // CHUNK, chunk, 0, unroll=4)


def _ssd(z, xbc, dt, conv_w, conv_b, dt_bias, a_log, d_skip, norm_g, batch, seq):
    t = z.shape[0]
    rb = SSD_BLOCK
    nblk = seq // rb
    tri = jnp.tril(jnp.ones((CHUNK, CHUNK), BF16))
    lane = jnp.arange(SSM_INNER)
    expand = (jnp.arange(LANES)[:, None] == (lane[None, :] // SSM_HEAD_DIM)).astype(BF16)
    pad_h = lambda v: jnp.pad(v.astype(F32), (0, LANES - SSM_HEADS)).reshape(1, LANES)
    row = lambda b, s: (b * nblk + s, 0)
    fixed = lambda b, s: (0, 0)
    return pl.pallas_call(
        _ssd_kernel,
        grid=(batch, nblk),
        in_specs=[pl.BlockSpec((rb, SSM_INNER), row), pl.BlockSpec((rb, SSM_CONV_CH), row),
                  pl.BlockSpec((rb, LANES), row),
                  pl.BlockSpec((SSM_CONV, SSM_CONV_CH), fixed), pl.BlockSpec((1, SSM_CONV_CH), fixed),
                  pl.BlockSpec((1, LANES), fixed), pl.BlockSpec((1, LANES), fixed),
                  pl.BlockSpec((1, SSM_INNER), fixed), pl.BlockSpec((1, SSM_INNER), fixed),
                  pl.BlockSpec((CHUNK, CHUNK), fixed), pl.BlockSpec((LANES, SSM_INNER), fixed)],
        out_specs=pl.BlockSpec((rb, SSM_INNER), row),
        out_shape=jax.ShapeDtypeStruct((t, SSM_INNER), BF16),
        scratch_shapes=[pltpu.VMEM((SSM_GROUPS, SSM_STATE, SSM_GROUP_WIDTH), F32),
                        pltpu.VMEM((rb + SUBLANES, SSM_CONV_CH), F32), pltpu.VMEM((rb, SSM_CONV_CH), F32)],
        compiler_params=_params("parallel", "arbitrary"),
        name="ssd_scan",
    )(z, xbc, dt, conv_w.astype(F32), conv_b.reshape(1, SSM_CONV_CH).astype(F32), pad_h(dt_bias), pad_h(a_log),
      jnp.repeat(d_skip.astype(F32), SSM_HEAD_DIM).reshape(1, SSM_INNER), norm_g.reshape(1, SSM_INNER).astype(F32),
      tri, expand)


def kernel(x, mix_norm, ffn_norm, ev_w_in, hg_lb_logits, hg_out_norm, at_q_norm, at_k_norm, ev_w_out, od_w_in,
           od_conv_w, od_conv_b, od_dt_bias, od_A_log, od_D, od_out_norm, od_w_out, moe_wg, moe_bg, moe_we,
           moe_be, moe_w1, moe_w3, moe_w2):
    batch, seq, d = x.shape
    assert d == D_MODEL and seq % HG_BLOCK == 0 and seq % KEY_TILE == 0 and (batch * seq) % ROUTER_TILE == 0
    depth = mix_norm.shape[0]
    x2d = x.reshape(batch * seq, d).astype(F32)
    lbs = jnp.cumsum(jax.nn.softmax(hg_lb_logits.astype(F32), axis=0), axis=0)
    for layer in range(depth):
        i = layer // 2
        if layer % 2 == 0:
            hproj, q, k, v, iq, ik, iw = _even_proj(x2d, mix_norm[layer], ev_w_in[i], at_q_norm[i], at_k_norm[i],
                                                    seq)
            o_h = _hgrn(hproj, lbs[i], hg_out_norm[i], batch, seq)
            o_a = _dsa(q, iq, iw, k, v, ik, batch, seq)
            w_h = ev_w_out[i][:HG_WIDTH].astype(BF16)
            w_a = jnp.pad(ev_w_out[i][HG_WIDTH:].reshape(AT_HEADS, AT_DIM, d),
                          ((0, 0), (0, LANES - AT_DIM), (0, 0))).reshape(AT_HEADS * LANES, d).astype(BF16)
            acts, weights = [o_h, o_a], [w_h, w_a]
        else:
            z, xbc, dt = _odd_proj(x2d, mix_norm[layer], od_w_in[i])
            y = _ssd(z, xbc, dt, od_conv_w[i], od_conv_b[i], od_dt_bias[i], od_A_log[i], od_D[i], od_out_norm[i],
                     batch, seq)
            acts, weights = [y], [od_w_out[i].astype(BF16)]
        x1, hn, route, counts = _out_router(x2d, acts, weights, ffn_norm[layer], moe_wg[layer], moe_bg[layer],
                                            moe_we[layer], moe_be[layer])
        x2d = _moe(x1, hn, route, counts, moe_w1, moe_w3, moe_w2, layer)
    return x2d.reshape(batch, seq, d).astype(x.dtype)
```
